```python
import math
import jax
import jax.numpy as jnp
from jax import lax
import numpy as np

D_MODEL = 4096
BATCH = 2
SEQ = 8192
DEPTH = 2
DEC_BATCH = 8
DEC_SEQ = 32
PAST_LEN = 1024

CHUNK = 64
Q_BLOCK = 128
N_MIXERS = 2
N_MLA_LAYERS = (DEPTH + 1) // 2
N_RET_LAYERS = DEPTH // 2
N_DENSE_LAYERS = (DEPTH + 1) // 2
N_MOE_LAYERS = DEPTH // 2
RMS_EPS = 1e-6

MLA_HEADS = 32
MLA_Q_RANK = 1024
MLA_KV_RANK = 512
MLA_NOPE = 128
MLA_ROPE = 64
MLA_V = 128
MLA_ROPE_BASE = 10000.0
MLA_SCALE = (MLA_NOPE + MLA_ROPE) ** -0.5

RET_HEADS = 16
RET_DK = D_MODEL // RET_HEADS
RET_DV = 2 * RET_DK
RET_ROPE_BASE = 10000.0

N_MEM = 256
XA_HEADS = 4
XA_HEAD_DIM = 128
XA_SCALE = XA_HEAD_DIM ** -0.5

D_FF = 14336
N_EXPERTS = 8
TOP_K = 2
MOE_BLOCK = 256

kernel_name = 'mla_retention_streaming_encoder_step'


def rms_norm(x, g=None):
    xf = x.astype(jnp.float32)
    y = xf * lax.rsqrt(jnp.mean(xf * xf, axis=-1, keepdims=True) + RMS_EPS)
    if g is not None:
        y = y * g.astype(jnp.float32)
    return y.astype(x.dtype)


def rope_half(x, pos):
    half = x.shape[-1] // 2
    inv = 1.0 / (MLA_ROPE_BASE ** (jnp.arange(half, dtype=jnp.float32) / half))
    ang = pos.astype(jnp.float32)[:, None] * inv[None, :]
    ang = ang.reshape((pos.shape[0],) + (1,) * (x.ndim - 3) + (half,))
    c, s = jnp.cos(ang), jnp.sin(ang)
    xf = x.astype(jnp.float32)
    x1, x2 = xf[..., :half], xf[..., half:]
    return jnp.concatenate([x1 * c - x2 * s, x2 * c + x1 * s], axis=-1).astype(x.dtype)


def rope_pairs(x, pos):
    half = x.shape[-1] // 2
    inv = 1.0 / (RET_ROPE_BASE ** jnp.linspace(0.0, 1.0, half, dtype=jnp.float32))
    ang = (pos.astype(jnp.float32)[:, None] * inv[None, :])[:, None, :]
    c, s = jnp.cos(ang), jnp.sin(ang)
    xr = x.astype(jnp.float32).reshape(x.shape[:-1] + (half, 2))
    xe, xo = xr[..., 0], xr[..., 1]
    return jnp.stack([xe * c - xo * s, xo * c + xe * s], axis=-1).reshape(x.shape)


def mla_project(xn, pos, w_dq, g_q, w_uq, w_dkv, g_kv):
    b, l, _ = xn.shape
    c_q = rms_norm(xn @ w_dq, g_q)
    q = (c_q @ w_uq).reshape(b, l, MLA_HEADS, MLA_NOPE + MLA_ROPE)
    q_nope, q_rope = q[..., :MLA_NOPE], rope_half(q[..., MLA_NOPE:], pos)
    c_kv = xn @ w_dkv
    ckv = rms_norm(c_kv[..., :MLA_KV_RANK], g_kv)
    k_rope = rope_half(c_kv[..., MLA_KV_RANK:], pos)
    return q_nope, q_rope, ckv, k_rope


def mla_keys(ckv, w_uk, w_uv):
    k_nope = jnp.einsum('bkc,chd->bkhd', ckv, w_uk)
    v = jnp.einsum('bkc,chd->bkhd', ckv, w_uv)
    return k_nope, v


def mla_attend(q_nope, q_rope, k_nope, k_rope, v, q_pos, k_pos):
    s = (jnp.einsum('bqhd,bkhd->bhqk', q_nope, k_nope)
         + jnp.einsum('bqhr,bkr->bhqk', q_rope, k_rope)).astype(jnp.float32) * MLA_SCALE
    visible = (k_pos[None, :] // CHUNK) <= (q_pos[:, None] // CHUNK)
    p = jax.nn.softmax(jnp.where(visible, s, -jnp.inf), axis=-1)
    return jnp.einsum('bhqk,bkhd->bqhd', p.astype(v.dtype), v)


def mla_prompt(xn, pos, w_dq, g_q, w_uq, w_dkv, g_kv, w_uk, w_uv, w_o):
    b, l, _ = xn.shape
    q_nope, q_rope, ckv, k_rope = mla_project(xn, pos, w_dq, g_q, w_uq, w_dkv, g_kv)
    k_nope, v = mla_keys(ckv, w_uk, w_uv)
    nb = l // Q_BLOCK

    def to_blocks(t):
        return jnp.moveaxis(t.reshape((b, nb, Q_BLOCK) + t.shape[2:]), 1, 0)

    o = lax.map(lambda a: mla_attend(a[0], a[1], k_nope, k_rope, v, a[2], pos),
                (to_blocks(q_nope), to_blocks(q_rope), pos.reshape(nb, Q_BLOCK)))
    o = jnp.moveaxis(o, 0, 1).reshape(b, l, MLA_HEADS * MLA_V)
    return o @ w_o, ckv, k_rope


def mla_sample(xn, pos, ckv_cache, krope_cache, w_dq, g_q, w_uq, w_dkv, g_kv, w_uk, w_uv, w_o):
    b, l, _ = xn.shape
    q_nope, q_rope, ckv, k_rope = mla_project(xn, pos, w_dq, g_q, w_uq, w_dkv, g_kv)
    ckv_all = jnp.concatenate([ckv_cache.astype(ckv.dtype), ckv], axis=1)
    kr_all = jnp.concatenate([krope_cache.astype(k_rope.dtype), k_rope], axis=1)
    k_nope, v = mla_keys(ckv_all, w_uk, w_uv)
    k_pos = jnp.arange(ckv_all.shape[1], dtype=jnp.int32)
    o = mla_attend(q_nope, q_rope, k_nope, kr_all, v, pos, k_pos).reshape(b, l, MLA_HEADS * MLA_V)
    return o @ w_o, ckv, k_rope


def ret_log_decay():
    return jnp.log1p(-jnp.exp2(-5.0 - jnp.arange(RET_HEADS, dtype=jnp.float32)))


def ret_project(xn, pos, w_q, w_k, w_v, w_g):
    b, l, _ = xn.shape
    q = rope_pairs((xn @ w_q).reshape(b, l, RET_HEADS, RET_DK), pos)
    k = rope_pairs((xn @ w_k).reshape(b, l, RET_HEADS, RET_DK), pos) * RET_DK ** -0.5
    v = (xn @ w_v).reshape(b, l, RET_HEADS, RET_DV).astype(jnp.float32)
    return q, k, v, xn @ w_g


def retention_chunk(state, q, k, v, log_g):
    l = q.shape[1]
    idx = jnp.arange(l, dtype=jnp.float32)
    diff = idx[:, None] - idx[None, :]
    dmat = jnp.exp(jnp.where(diff[None] >= 0, diff[None] * log_g[:, None, None], -jnp.inf))
    scores = jnp.einsum('bihd,bjhd->bhij', q, k) * dmat
    intra = jnp.einsum('bhij,bjhe->bihe', scores, v)
    q_decay = jnp.exp((idx[:, None] + 1.0) * log_g[None, :])
    inter = jnp.einsum('bihd,bhde->bihe', q, state) * q_decay[None, :, :, None]
    k_decay = jnp.exp((l - 1.0 - idx)[:, None] * log_g[None, :])
    new_state = (jnp.exp(l * log_g)[None, :, None, None] * state
                 + jnp.einsum('bjhd,bjhe->bhde', k * k_decay[None, :, :, None], v))
    return new_state, intra + inter


def ret_output(o, g, w_o):
    b, l = o.shape[:2]
    o = rms_norm(o).reshape(b, l, RET_HEADS * RET_DV).astype(g.dtype)
    return (jax.nn.silu(g) * o) @ w_o


def ret_prompt(xn, pos, w_q, w_k, w_v, w_g, w_o):
    b, l, _ = xn.shape
    q, k, v, g = ret_project(xn, pos, w_q, w_k, w_v, w_g)
    log_g = ret_log_decay()
    nc = l // CHUNK

    def to_chunks(t):
        return jnp.moveaxis(t.reshape((b, nc, CHUNK) + t.shape[2:]), 1, 0)

    s0 = jnp.zeros((b, RET_HEADS, RET_DK, RET_DV), jnp.float32)
    s_fin, o = lax.scan(lambda s, a: retention_chunk(s, a[0], a[1], a[2], log_g), s0,
                        (to_chunks(q), to_chunks(k), to_chunks(v)))
    o = jnp.moveaxis(o, 0, 1).reshape(b, l, RET_HEADS, RET_DV)
    return ret_output(o, g, w_o), s_fin


def ret_sample(xn, pos, state, w_q, w_k, w_v, w_g, w_o):
    q, k, v, g = ret_project(xn, pos, w_q, w_k, w_v, w_g)
    s_new, o = retention_chunk(state.astype(jnp.float32), q, k, v, ret_log_decay())
    return ret_output(o, g, w_o), s_new


def mem_kv(mem, w_k, w_v):
    b, n, _ = mem.shape
    k = (mem @ w_k).reshape(b, n, XA_HEADS, XA_HEAD_DIM)
    v = (mem @ w_v).reshape(b, n, XA_HEADS, XA_HEAD_DIM)
    return k, v


def mem_attend(xn, k, v, w_q, w_o):
    b, l, _ = xn.shape
    q = (xn @ w_q).reshape(b, l, XA_HEADS, XA_HEAD_DIM)
    s = jnp.einsum('bqhd,bmhd->bhqm', q, k.astype(q.dtype)).astype(jnp.float32) * XA_SCALE
    p = jax.nn.softmax(s, axis=-1).astype(q.dtype)
    o = jnp.einsum('bhqm,bmhd->bqhd', p, v.astype(q.dtype)).reshape(b, l, XA_HEADS * XA_HEAD_DIM)
    return o @ w_o


def swiglu(x, w1, w3, w2):
    return (jax.nn.silu(x @ w1) * (x @ w3)) @ w2


def moe_swiglu(x2d, w_router, w1, w3, w2):
    t, d = x2d.shape
    logits = (x2d @ w_router).astype(jnp.float32)
    top_val, top_idx = lax.top_k(logits, TOP_K)
    gate = jax.nn.softmax(top_val, axis=-1)
    n_assign = t * TOP_K
    flat_e = top_idx.reshape(-1)
    flat_tok = jnp.arange(n_assign, dtype=jnp.int32) // TOP_K
    flat_gate = gate.reshape(-1)
    order = jnp.argsort(flat_e)
    sorted_e = flat_e[order]
    counts = jnp.bincount(flat_e, length=N_EXPERTS)
    padded = (counts + MOE_BLOCK - 1) // MOE_BLOCK * MOE_BLOCK
    pad_end = jnp.cumsum(padded)
    pad_start = pad_end - padded
    start = jnp.cumsum(counts) - counts
    rank = jnp.arange(n_assign, dtype=jnp.int32) - start[sorted_e]
    dest = pad_start[sorted_e] + rank
    n_blocks = (n_assign + N_EXPERTS * (MOE_BLOCK - 1)) // MOE_BLOCK
    n_slots = n_blocks * MOE_BLOCK
    slot_tok = jnp.full((n_slots,), t, jnp.int32).at[dest].set(flat_tok[order])
    slot_gate = jnp.zeros((n_slots,), jnp.float32).at[dest].set(flat_gate[order])
    block_start = jnp.arange(n_blocks, dtype=pad_end.dtype) * MOE_BLOCK
    block_e = jnp.minimum(jnp.searchsorted(pad_end, block_start, side='right'), N_EXPERTS - 1)
    x_pad = jnp.concatenate([x2d, jnp.zeros((1, d), x2d.dtype)], axis=0)
    xb = x_pad[slot_tok].reshape(n_blocks, MOE_BLOCK, d)

    def expert_block(a):
        xs, e = a
        return (jax.nn.silu(xs @ w1[e]) * (xs @ w3[e])) @ w2[e]

    yb = lax.map(expert_block, (xb, block_e)).reshape(n_slots, d)
    y = jnp.zeros((t + 1, d), x2d.dtype).at[slot_tok].add(yb * slot_gate[:, None].astype(x2d.dtype))
    return y[:t]


def setup_inputs(seed: int = 0) -> dict:
    key = jax.random.key(seed)
    ks = jax.random.split(key, 40)
    counter = iter(range(40))
    d = D_MODEL

    def normal(shape, scale=1.0):
        return jax.random.normal(ks[next(counter)], shape, jnp.float32) * scale

    def gain(shape):
        return 1.0 + 0.02 * jax.random.normal(ks[next(counter)], shape, jnp.float32)

    return {
        'x_prompt': normal((BATCH, SEQ, d)),
        'x_sample': normal((DEC_BATCH, DEC_SEQ, d)),
        'mem_prompt': normal((BATCH, N_MEM, d)),
        'cache_mla_ckv': normal((N_MLA_LAYERS, DEC_BATCH, PAST_LEN, MLA_KV_RANK)),
        'cache_mla_krope': normal((N_MLA_LAYERS, DEC_BATCH, PAST_LEN, MLA_ROPE)),
        'state_ret': normal((N_RET_LAYERS, DEC_BATCH, RET_HEADS, RET_DK, RET_DV)),
        'cache_mem_k': normal((DEPTH, DEC_BATCH, N_MEM, XA_HEADS, XA_HEAD_DIM)),
        'cache_mem_v': normal((DEPTH, DEC_BATCH, N_MEM, XA_HEADS, XA_HEAD_DIM)),
        'norm_mix': gain((DEPTH, d)),
        'norm_xattn': gain((DEPTH, d)),
        'norm_ffn': gain((DEPTH, d)),
        'norm_final': gain((d,)),
        'mla_w_dq': normal((N_MLA_LAYERS, d, MLA_Q_RANK), d ** -0.5),
        'mla_g_q': gain((N_MLA_LAYERS, MLA_Q_RANK)),
        'mla_w_uq': normal((N_MLA_LAYERS, MLA_Q_RANK, MLA_HEADS * (MLA_NOPE + MLA_ROPE)), MLA_Q_RANK ** -0.5),
        'mla_w_dkv': normal((N_MLA_LAYERS, d, MLA_KV_RANK + MLA_ROPE), d ** -0.5),
        'mla_g_kv': gain((N_MLA_LAYERS, MLA_KV_RANK)),
        'mla_w_uk': normal((N_MLA_LAYERS, MLA_KV_RANK, MLA_HEADS, MLA_NOPE), MLA_KV_RANK ** -0.5),
        'mla_w_uv': normal((N_MLA_LAYERS, MLA_KV_RANK, MLA_HEADS, MLA_V), MLA_KV_RANK ** -0.5),
        'mla_w_o': normal((N_MLA_LAYERS, MLA_HEADS * MLA_V, d), (MLA_HEADS * MLA_V) ** -0.5),
        'ret_w_q': normal((N_RET_LAYERS, d, RET_HEADS * RET_DK), d ** -0.5),
        'ret_w_k': normal((N_RET_LAYERS, d, RET_HEADS * RET_DK), d ** -0.5),
        'ret_w_v': normal((N_RET_LAYERS, d, RET_HEADS * RET_DV), d ** -0.5),
        'ret_w_g': normal((N_RET_LAYERS, d, RET_HEADS * RET_DV), d ** -0.5),
        'ret_w_o': normal((N_RET_LAYERS, RET_HEADS * RET_DV, d), (RET_HEADS * RET_DV) ** -0.5),
        'xa_w_q': normal((DEPTH, d, XA_HEADS * XA_HEAD_DIM), d ** -0.5),
        'xa_w_k': normal((DEPTH, d, XA_HEADS * XA_HEAD_DIM), d ** -0.5),
        'xa_w_v': normal((DEPTH, d, XA_HEADS * XA_HEAD_DIM), d ** -0.5),
        'xa_w_o': normal((DEPTH, XA_HEADS * XA_HEAD_DIM, d), (XA_HEADS * XA_HEAD_DIM) ** -0.5),
        'ffn_w1': normal((N_DENSE_LAYERS, d, D_FF), d ** -0.5),
        'ffn_w3': normal((N_DENSE_LAYERS, d, D_FF), d ** -0.5),
        'ffn_w2': normal((N_DENSE_LAYERS, D_FF, d), D_FF ** -0.5),
        'moe_w_router': normal((N_MOE_LAYERS, d, N_EXPERTS), d ** -0.5),
        'moe_w1': normal((N_MOE_LAYERS, N_EXPERTS, d, D_FF), d ** -0.5),
        'moe_w3': normal((N_MOE_LAYERS, N_EXPERTS, d, D_FF), d ** -0.5),
        'moe_w2': normal((N_MOE_LAYERS, N_EXPERTS, D_FF, d), D_FF ** -0.5),
    }


def reference(x_prompt, x_sample, mem_prompt, cache_mla_ckv, cache_mla_krope, state_ret,
              cache_mem_k, cache_mem_v, norm_mix, norm_xattn, norm_ffn, norm_final,
              mla_w_dq, mla_g_q, mla_w_uq, mla_w_dkv, mla_g_kv, mla_w_uk, mla_w_uv, mla_w_o,
              ret_w_q, ret_w_k, ret_w_v, ret_w_g, ret_w_o,
              xa_w_q, xa_w_k, xa_w_v, xa_w_o,
              ffn_w1, ffn_w3, ffn_w2,
              moe_w_router, moe_w1, moe_w3, moe_w2):
    b_p, l_p, d = x_prompt.shape
    b_s, l_s, _ = x_sample.shape
    past = cache_mla_ckv.shape[2]
    pos_p = jnp.arange(l_p, dtype=jnp.int32)
    pos_s = past + jnp.arange(l_s, dtype=jnp.int32)
    hp, hs = x_prompt, x_sample
    ckv_p, kr_p, ckv_s, kr_s = [], [], [], []
    ret_p, ret_s = [], []
    mem_k_p, mem_v_p = [], []

    for i in range(DEPTH):
        jm = i // N_MIXERS
        xp, xs = rms_norm(hp, norm_mix[i]), rms_norm(hs, norm_mix[i])
        if i % N_MIXERS == 0:
            wa = (mla_w_dq[jm], mla_g_q[jm], mla_w_uq[jm], mla_w_dkv[jm], mla_g_kv[jm],
                  mla_w_uk[jm], mla_w_uv[jm], mla_w_o[jm])
            yp, c_p, r_p = mla_prompt(xp, pos_p, *wa)
            ys, c_s, r_s = mla_sample(xs, pos_s, cache_mla_ckv[jm], cache_mla_krope[jm], *wa)
            ckv_p.append(c_p)
            kr_p.append(r_p)
            ckv_s.append(c_s.astype(cache_mla_ckv.dtype))
            kr_s.append(r_s.astype(cache_mla_krope.dtype))
        else:
            wb = (ret_w_q[jm], ret_w_k[jm], ret_w_v[jm], ret_w_g[jm], ret_w_o[jm])
            yp, st_p = ret_prompt(xp, pos_p, *wb)
            ys, st_s = ret_sample(xs, pos_s, state_ret[jm], *wb)
            ret_p.append(st_p.astype(x_prompt.dtype))
            ret_s.append(st_s.astype(state_ret.dtype))
        hp, hs = hp + yp, hs + ys

        mk, mv = mem_kv(mem_prompt, xa_w_k[i], xa_w_v[i])
        mem_k_p.append(mk)
        mem_v_p.append(mv)
        hp = hp + mem_attend(rms_norm(hp, norm_xattn[i]), mk, mv, xa_w_q[i], xa_w_o[i])
        hs = hs + mem_attend(rms_norm(hs, norm_xattn[i]), cache_mem_k[i], cache_mem_v[i], xa_w_q[i], xa_w_o[i])

        jf = i // 2
        xp, xs = rms_norm(hp, norm_ffn[i]), rms_norm(hs, norm_ffn[i])
        if i % 2 == 0:
            hp = hp + swiglu(xp, ffn_w1[jf], ffn_w3[jf], ffn_w2[jf])
            hs = hs + swiglu(xs, ffn_w1[jf], ffn_w3[jf], ffn_w2[jf])
        else:
            moe_w = (moe_w_router[jf], moe_w1[jf], moe_w3[jf], moe_w2[jf])
            hp = hp + moe_swiglu(xp.reshape(b_p * l_p, d), *moe_w).reshape(b_p, l_p, d)
            hs = hs + moe_swiglu(xs.reshape(b_s * l_s, d), *moe_w).reshape(b_s, l_s, d)

    y_prompt = rms_norm(hp, norm_final)
    y_sample = rms_norm(hs, norm_final)
    return (y_prompt, y_sample,
            jnp.stack(ckv_p), jnp.stack(kr_p), jnp.stack(ret_p),
            jnp.stack(mem_k_p), jnp.stack(mem_v_p),
            jnp.stack(ckv_s), jnp.stack(kr_s), jnp.stack(ret_s))
```

```python
import functools

import jax
import jax.numpy as jnp
from jax import lax
from jax.experimental import pallas as pl
from jax.experimental.pallas import tpu as pltpu

F32 = jnp.float32
BF16 = jnp.bfloat16

RMS_EPS = 1e-6
CHUNK = 64
MLA_NOPE = 128
MLA_ROPE = 64
MLA_V = 128
MLA_ROPE_BASE = 10000.0
RET_HEADS = 16
RET_ROPE_BASE = 10000.0
XA_HEADS = 4
XA_HEAD_DIM = 128
N_EXPERTS = 8
TOP_K = 2
MOE_BLOCK = 256
LANES = 128
NEG_BIG = -1e30
VMEM_LIMIT_BYTES = 56 * 1024 * 1024


def _params(*sem):
    return pltpu.CompilerParams(dimension_semantics=sem, vmem_limit_bytes=VMEM_LIMIT_BYTES)


def _tile(n, candidates):
    for c in candidates:
        if c <= n and n % c == 0:
            return c
    return n


def _rms(x, g=None):
    y = x * lax.rsqrt(jnp.mean(x * x, axis=-1, keepdims=True) + RMS_EPS)
    return y if g is None else y * g


def _rmsnorm_kernel(x_ref, g_ref, o_ref):
    o_ref[...] = _rms(x_ref[...], g_ref[...]).astype(o_ref.dtype)


def rmsnorm(x, g, out_dtype):
    t, d = x.shape
    tm = _tile(t, (256, 128, 64, 32, 16, 8))
    return pl.pallas_call(
        _rmsnorm_kernel,
        out_shape=jax.ShapeDtypeStruct((t, d), out_dtype),
        grid=(t // tm,),
        in_specs=[pl.BlockSpec((tm, d), lambda i: (i, 0)), pl.BlockSpec((1, d), lambda i: (0, 0))],
        out_specs=pl.BlockSpec((tm, d), lambda i: (i, 0)),
        compiler_params=_params("parallel"),
        name="rmsnorm",
    )(x, g.reshape(1, d))


def _mm_kernel(*refs, nk, has_res):
    a_ref, w_ref = refs[0], refs[1]
    res_ref = refs[2] if has_res else None
    o_ref = refs[2 + has_res]
    part = jnp.dot(a_ref[...], w_ref[...], preferred_element_type=F32)
    if nk == 1:
        if has_res:
            part = part + res_ref[...]
        o_ref[...] = part.astype(o_ref.dtype)
        return
    acc_ref = refs[3 + has_res]
    k = pl.program_id(2)

    @pl.when(k == 0)
    def _():
        acc_ref[...] = part

    @pl.when(k > 0)
    def _():
        acc_ref[...] += part

    @pl.when(k == nk - 1)
    def _():
        out = acc_ref[...]
        if has_res:
            out = out + res_ref[...]
        o_ref[...] = out.astype(o_ref.dtype)


def matmul(a, w, *, out_dtype, res=None, tm=None, tn=None, tk=None, name="matmul"):
    m, kdim = a.shape
    n = w.shape[1]
    tk = tk or (kdim if kdim <= 4096 else _tile(kdim, (2048, 1024, 512)))
    row_tiles = (1280, 1024) if tk <= 2048 else ()
    tm = tm or _tile(m, row_tiles + (640, 512, 256, 128, 64, 32, 16, 8))
    tn = tn or _tile(n, (1024, 512, 256, 128))
    nk = kdim // tk
    has_res = res is not None
    in_specs = [pl.BlockSpec((tm, tk), lambda j, i, k: (i, k)),
                pl.BlockSpec((tk, tn), lambda j, i, k: (k, j))]
    args = [a, w]
    if has_res:
        in_specs.append(pl.BlockSpec((tm, tn), lambda j, i, k: (i, j)))
        args.append(res)
    scratch = [pltpu.VMEM((tm, tn), F32)] if nk > 1 else []
    return pl.pallas_call(
        functools.partial(_mm_kernel, nk=nk, has_res=has_res),
        out_shape=jax.ShapeDtypeStruct((m, n), out_dtype),
        grid=(n // tn, m // tm, nk),
        in_specs=in_specs,
        out_specs=pl.BlockSpec((tm, tn), lambda j, i, k: (i, j)),
        scratch_shapes=scratch,
        compiler_params=_params("parallel", "parallel", "arbitrary"),
        name=name,
    )(*args)


def _mla_pre_kernel(x_ref, w_ref, gq_ref, gkv_ref, cos_ref, sin_ref,
                    cq_ref, ckv_ref, ckvb_ref, kr_ref, krb_ref, *, q_rank, kv_rank):
    r = jnp.dot(x_ref[...], w_ref[...], preferred_element_type=F32)
    cq_ref[...] = _rms(r[:, :q_rank], gq_ref[...]).astype(cq_ref.dtype)
    ckv = _rms(r[:, q_rank:q_rank + kv_rank], gkv_ref[...])
    ckv_ref[...] = ckv
    ckvb_ref[...] = ckv.astype(ckvb_ref.dtype)
    o = q_rank + kv_rank
    kr = r[:, o:o + LANES] * cos_ref[...] + r[:, o + LANES:o + 2 * LANES] * sin_ref[...]
    kr_ref[...] = kr[:, :MLA_ROPE]
    krb_ref[...] = kr.astype(krb_ref.dtype)


def mla_pre(xn, w_pre, g_q, g_kv, cos_t, sin_t):
    t, d = xn.shape
    q_rank, kv_rank = g_q.shape[0], g_kv.shape[0]
    n = w_pre.shape[1]
    tm = _tile(t, (256, 128, 64, 32, 16, 8))
    row = lambda i: (i, 0)
    fixed = lambda i: (0, 0)
    return pl.pallas_call(
        functools.partial(_mla_pre_kernel, q_rank=q_rank, kv_rank=kv_rank),
        out_shape=(jax.ShapeDtypeStruct((t, q_rank), BF16),
                   jax.ShapeDtypeStruct((t, kv_rank), F32),
                   jax.ShapeDtypeStruct((t, kv_rank), BF16),
                   jax.ShapeDtypeStruct((t, MLA_ROPE), F32),
                   jax.ShapeDtypeStruct((t, LANES), BF16)),
        grid=(t // tm,),
        in_specs=[pl.BlockSpec((tm, d), row), pl.BlockSpec((d, n), fixed),
                  pl.BlockSpec((1, q_rank), fixed), pl.BlockSpec((1, kv_rank), fixed),
                  pl.BlockSpec((tm, LANES), row), pl.BlockSpec((tm, LANES), row)],
        out_specs=(pl.BlockSpec((tm, q_rank), row), pl.BlockSpec((tm, kv_rank), row),
                   pl.BlockSpec((tm, kv_rank), row), pl.BlockSpec((tm, MLA_ROPE), row),
                   pl.BlockSpec((tm, LANES), row)),
        compiler_params=_params("parallel"),
        name="mla_pre",
    )(xn, w_pre, g_q.reshape(1, -1), g_kv.reshape(1, -1), cos_t, sin_t)


def _mla_q_kernel(c_ref, wn_ref, wr_ref, wrot_ref, cos_ref, sin_ref, q_ref, *, heads_per_step):
    c = c_ref[...]
    qn = jnp.dot(c, wn_ref[...], preferred_element_type=F32)
    qr = jnp.dot(c, wr_ref[...], preferred_element_type=F32)
    qrot = jnp.dot(c, wrot_ref[...], preferred_element_type=F32)
    cos, sin = cos_ref[...], sin_ref[...]
    for h in range(heads_per_step):
        lo, hi = h * LANES, (h + 1) * LANES
        q_ref[:, 2 * lo:2 * lo + LANES] = qn[:, lo:hi].astype(q_ref.dtype)
        q_ref[:, 2 * lo + LANES:2 * hi] = (qr[:, lo:hi] * cos + qrot[:, lo:hi] * sin).astype(q_ref.dtype)


def mla_q(cq, wn, wr, wrot, cos_t, sin_t):
    t, q_rank = cq.shape
    n = wn.shape[1]
    tm = _tile(t, (1280, 1024, 640, 512, 256, 128, 64, 32, 16, 8))
    tn = _tile(n, (512, 256, 128))
    w_spec = pl.BlockSpec((q_rank, tn), lambda j, i: (0, j))
    return pl.pallas_call(
        functools.partial(_mla_q_kernel, heads_per_step=tn // LANES),
        out_shape=jax.ShapeDtypeStruct((t, 2 * n), BF16),
        grid=(n // tn, t // tm),
        in_specs=[pl.BlockSpec((tm, q_rank), lambda j, i: (i, 0)), w_spec, w_spec, w_spec,
                  pl.BlockSpec((tm, LANES), lambda j, i: (i, 0)),
                  pl.BlockSpec((tm, LANES), lambda j, i: (i, 0))],
        out_specs=pl.BlockSpec((tm, 2 * tn), lambda j, i: (i, j)),
        compiler_params=_params("parallel", "parallel"),
        name="mla_q",
    )(cq, wn, wr, wrot, cos_t, sin_t)


def _flash_last_k(qi, *, tq, tk, nk, q_off):
    last_pos = ((q_off + (qi + 1) * tq - 1) // CHUNK) * CHUNK + CHUNK - 1
    return jnp.minimum(nk - 1, last_pos // tk)


def _flash_kernel(q_ref, kn_ref, kr_ref, v_ref, o_ref, m_scr, l_scr, acc_scr,
                  *, tq, tk, nk, q_off, lk_valid, scale):
    qi, ki = pl.program_id(2), pl.program_id(3)
    last = _flash_last_k(qi, tq=tq, tk=tk, nk=nk, q_off=q_off)

    @pl.when(ki == 0)
    def _():
        m_scr[...] = jnp.full(m_scr.shape, NEG_BIG, F32)
        l_scr[...] = jnp.zeros(l_scr.shape, F32)
        acc_scr[...] = jnp.zeros(acc_scr.shape, F32)

    @pl.when(ki <= last)
    def _():
        k = jnp.concatenate([kn_ref[...], kr_ref[...]], axis=-1)
        s = lax.dot_general(q_ref[...], k, (((1,), (1,)), ((), ())), preferred_element_type=F32) * scale
        qpos = q_off + qi * tq + lax.broadcasted_iota(jnp.int32, (tq, tk), 0)
        kpos = ki * tk + lax.broadcasted_iota(jnp.int32, (tq, tk), 1)
        visible = (kpos // CHUNK) <= (qpos // CHUNK)
        s = jnp.where(visible, s, NEG_BIG)
        if lk_valid < nk * tk:
            s = jnp.where(kpos < lk_valid, s, NEG_BIG)
        m_prev = m_scr[...]
        m_new = jnp.maximum(m_prev, jnp.max(s, axis=-1, keepdims=True))
        alpha = jnp.exp(m_prev - m_new)
        p = jnp.exp(s - m_new)
        l_scr[...] = alpha * l_scr[...] + jnp.sum(p, axis=-1, keepdims=True)
        acc_scr[...] = alpha * acc_scr[...] + jnp.dot(p.astype(BF16), v_ref[...], preferred_element_type=F32)
        m_scr[...] = m_new

    @pl.when(ki == nk - 1)
    def _():
        o_ref[...] = (acc_scr[...] / l_scr[...]).astype(o_ref.dtype)


def flash_attention(q, kv, kr, *, batch, heads, lq, lk, lk_valid, q_off, tq, tk):
    nq, nk = lq // tq, lk // tk
    last_k = functools.partial(_flash_last_k, tq=tq, tk=tk, nk=nk, q_off=q_off)
    scale = float((MLA_NOPE + MLA_ROPE) ** -0.5)

    def kv_row(b, qi, ki):
        return b * nk + jnp.minimum(ki, last_k(qi))

    return pl.pallas_call(
        functools.partial(_flash_kernel, tq=tq, tk=tk, nk=nk, q_off=q_off, lk_valid=lk_valid, scale=scale),
        out_shape=jax.ShapeDtypeStruct((batch * lq, heads * MLA_V), BF16),
        grid=(batch, heads, nq, nk),
        in_specs=[pl.BlockSpec((tq, 2 * LANES), lambda b, h, qi, ki: (b * nq + qi, h)),
                  pl.BlockSpec((tk, LANES), lambda b, h, qi, ki: (kv_row(b, qi, ki), h)),
                  pl.BlockSpec((tk, LANES), lambda b, h, qi, ki: (kv_row(b, qi, ki), 0)),
                  pl.BlockSpec((tk, LANES), lambda b, h, qi, ki: (kv_row(b, qi, ki), heads + h))],
        out_specs=pl.BlockSpec((tq, MLA_V), lambda b, h, qi, ki: (b * nq + qi, h)),
        scratch_shapes=[pltpu.VMEM((tq, 1), F32), pltpu.VMEM((tq, 1), F32), pltpu.VMEM((tq, MLA_V), F32)],
        compiler_params=_params("parallel", "parallel", "parallel", "arbitrary"),
        name="mla_flash",
    )(q, kv, kr, kv)


def _xattn_attend(q, k, v):
    scale = float(XA_HEAD_DIM ** -0.5)
    outs = []
    for hd in range(XA_HEADS):
        sl = slice(hd * XA_HEAD_DIM, (hd + 1) * XA_HEAD_DIM)
        s = lax.dot_general(q[:, sl], k[:, sl], (((1,), (1,)), ((), ())), preferred_element_type=F32) * scale
        p = jnp.exp(s - jnp.max(s, axis=-1, keepdims=True))
        p = p / jnp.sum(p, axis=-1, keepdims=True)
        outs.append(jnp.dot(p.astype(BF16), v[:, sl], preferred_element_type=F32).astype(BF16))
    return jnp.concatenate(outs, axis=-1)


def _xattn_kernel(*refs, sub, n_long_tiles):
    h_ref, gx_ref, wq_ref, wo_ref, gf_ref = refs[:5]
    k_refs, v_refs = refs[5:5 + sub], refs[5 + sub:5 + 2 * sub]
    h_out, xn_out, o_scr = refs[5 + 2 * sub:]
    i = pl.program_id(0)
    x = h_ref[...]
    xn = _rms(x, gx_ref[...]).astype(BF16)
    q = jnp.dot(xn, wq_ref[...], preferred_element_type=F32).astype(BF16)
    rows = q.shape[0] // sub

    @pl.when(i < n_long_tiles)
    def _():
        o_scr[...] = _xattn_attend(q, k_refs[0][...], v_refs[0][...])

    @pl.when(i >= n_long_tiles)
    def _():
        for j in range(sub):
            o_scr[j * rows:(j + 1) * rows, :] = _xattn_attend(q[j * rows:(j + 1) * rows], k_refs[j][...], v_refs[j][...])

    h_new = x + jnp.dot(o_scr[...], wo_ref[...], preferred_element_type=F32)
    h_out[...] = h_new
    xn_out[...] = _rms(h_new, gf_ref[...]).astype(xn_out.dtype)


def xattn_block(h, g_x, w_q, k, v, w_o, g_f, *, n_long, len_long, len_short, n_mem, tm, xn_dtype):
    t, d = h.shape
    hd = XA_HEADS * XA_HEAD_DIM
    sub = tm // len_short
    tiles_per_long = len_long // tm
    n_long_tiles = n_long * tiles_per_long
    row = lambda i: (i, 0)
    fixed = lambda i: (0, 0)

    def mem(j):
        def index(i):
            short = n_long + (i - n_long_tiles) * sub + j
            return (jnp.where(i < n_long_tiles, i // tiles_per_long, short), 0)
        return pl.BlockSpec((n_mem, hd), index)

    in_specs = ([pl.BlockSpec((tm, d), row), pl.BlockSpec((1, d), fixed), pl.BlockSpec((d, hd), fixed),
                 pl.BlockSpec((hd, d), fixed), pl.BlockSpec((1, d), fixed)]
                + [mem(j) for j in range(sub)] * 2)
    return pl.pallas_call(
        functools.partial(_xattn_kernel, sub=sub, n_long_tiles=n_long_tiles),
        out_shape=(jax.ShapeDtypeStruct((t, d), F32), jax.ShapeDtypeStruct((t, d), xn_dtype)),
        grid=(t // tm,),
        in_specs=in_specs,
        out_specs=(pl.BlockSpec((tm, d), row), pl.BlockSpec((tm, d), row)),
        scratch_shapes=[pltpu.VMEM((tm, hd), BF16)],
        compiler_params=_params("parallel"),
        name="xattn",
    )(h, g_x.reshape(1, d), w_q, w_o, g_f.reshape(1, d), *([k] * sub), *([v] * sub))


def _gateup_kernel(be_ref, na_ref, x_ref, w1_ref, w3_ref, o_ref):
    i = pl.program_id(1)

    @pl.when(i < na_ref[0])
    def _():
        x = x_ref[...]
        a = jnp.dot(x, w1_ref[...], preferred_element_type=F32)
        b = jnp.dot(x, w3_ref[...], preferred_element_type=F32)
        o_ref[...] = (a * jax.nn.sigmoid(a) * b).astype(o_ref.dtype)

    @pl.when(i >= na_ref[0])
    def _():
        o_ref[...] = jnp.zeros(o_ref.shape, o_ref.dtype)


def expert_gateup(xs, w1, w3, block_e, n_active, *, tm, tf):
    s, d = xs.shape
    f = w1.shape[2]
    w_spec = pl.BlockSpec((None, d, tf), lambda j, i, be, na: (be[i], 0, j))
    grid_spec = pltpu.PrefetchScalarGridSpec(
        num_scalar_prefetch=2,
        grid=(f // tf, s // tm),
        in_specs=[pl.BlockSpec((tm, d), lambda j, i, be, na: (i, 0)), w_spec, w_spec],
        out_specs=pl.BlockSpec((tm, tf), lambda j, i, be, na: (i, j)),
    )
    return pl.pallas_call(
        _gateup_kernel,
        out_shape=jax.ShapeDtypeStruct((s, f), BF16),
        grid_spec=grid_spec,
        compiler_params=_params("parallel", "arbitrary"),
        name="expert_gateup",
    )(block_e, n_active, xs, w1, w3)


def _down_kernel(be_ref, na_ref, h_ref, w_ref, o_ref):
    i = pl.program_id(1)

    @pl.when(i < na_ref[0])
    def _():
        o_ref[...] = jnp.dot(h_ref[...], w_ref[...], preferred_element_type=F32)

    @pl.when(i >= na_ref[0])
    def _():
        o_ref[...] = jnp.zeros(o_ref.shape, o_ref.dtype)


def expert_down(hs, w2, block_e, n_active, *, tm, tn):
    s, f = hs.shape
    d = w2.shape[2]
    grid_spec = pltpu.PrefetchScalarGridSpec(
        num_scalar_prefetch=2,
        grid=(d // tn, s // tm),
        in_specs=[pl.BlockSpec((tm, f), lambda j, i, be, na: (i, 0)),
                  pl.BlockSpec((None, f, tn), lambda j, i, be, na: (be[i], 0, j))],
        out_specs=pl.BlockSpec((tm, tn), lambda j, i, be, na: (i, j)),
    )
    return pl.pallas_call(
        _down_kernel,
        out_shape=jax.ShapeDtypeStruct((s, d), F32),
        grid_spec=grid_spec,
        compiler_params=_params("parallel", "arbitrary"),
        name="expert_down",
    )(block_e, n_active, hs, w2)


def _router_kernel(x_ref, wh_ref, wl_ref, idx_ref, gate_ref):
    x = x_ref[...]
    xh = x.astype(BF16)
    xl = (x - xh.astype(F32)).astype(BF16)
    wh, wl = wh_ref[...], wl_ref[...]
    logits = (jnp.dot(xh, wh, preferred_element_type=F32) + jnp.dot(xl, wh, preferred_element_type=F32)
              + jnp.dot(xh, wl, preferred_element_type=F32))
    lane = lax.broadcasted_iota(jnp.int32, logits.shape, 1)
    logits = jnp.where(lane < N_EXPERTS, logits, -jnp.inf)
    m1 = jnp.max(logits, axis=-1, keepdims=True)
    i1 = jnp.min(jnp.where(logits == m1, lane, LANES), axis=-1, keepdims=True)
    rest = jnp.where(lane == i1, -jnp.inf, logits)
    m2 = jnp.max(rest, axis=-1, keepdims=True)
    i2 = jnp.min(jnp.where(rest == m2, lane, LANES), axis=-1, keepdims=True)
    e = jnp.exp(m2 - m1)
    g1 = 1.0 / (1.0 + e)
    g2 = e / (1.0 + e)
    idx_ref[...] = jnp.where(lane == 0, i1, jnp.where(lane == 1, i2, 0))
    gate_ref[...] = jnp.where(lane == 0, g1, jnp.where(lane == 1, g2, 0.0))


def moe_router(xn, w_router):
    t, d = xn.shape
    w = jnp.zeros((d, LANES), F32).at[:, :N_EXPERTS].set(w_router)
    wh = w.astype(BF16)
    wl = (w - wh.astype(F32)).astype(BF16)
    tm = _tile(t, (256, 128, 64, 32, 16, 8))
    row = lambda i: (i, 0)
    fixed = lambda i: (0, 0)
    return pl.pallas_call(
        _router_kernel,
        out_shape=(jax.ShapeDtypeStruct((t, LANES), jnp.int32), jax.ShapeDtypeStruct((t, LANES), F32)),
        grid=(t // tm,),
        in_specs=[pl.BlockSpec((tm, d), row), pl.BlockSpec((d, LANES), fixed), pl.BlockSpec((d, LANES), fixed)],
        out_specs=(pl.BlockSpec((tm, LANES), row), pl.BlockSpec((tm, LANES), row)),
        compiler_params=_params("parallel"),
        name="moe_router",
    )(xn, wh, wl)


def _gather_kernel(tok_ref, x_hbm, o_ref, buf, sem, *, tm):
    base = pl.program_id(0) * tm

    def row_copy(r):
        return pltpu.make_async_copy(x_hbm.at[pl.ds(tok_ref[base + r], 1)], buf.at[pl.ds(r, 1)], sem)

    def start(r, c):
        row_copy(r).start()
        return c

    def wait(r, c):
        row_copy(r).wait()
        return c

    lax.fori_loop(0, tm, start, 0)
    lax.fori_loop(0, tm, wait, 0)
    o_ref[...] = buf[...].astype(o_ref.dtype)


def moe_gather(xn, slot_tok, *, tm):
    s = slot_tok.shape[0]
    d = xn.shape[1]
    grid_spec = pltpu.PrefetchScalarGridSpec(
        num_scalar_prefetch=1,
        grid=(s // tm,),
        in_specs=[pl.BlockSpec(memory_space=pl.ANY)],
        out_specs=pl.BlockSpec((tm, d), lambda i, tok: (i, 0)),
        scratch_shapes=[pltpu.VMEM((tm, d), F32), pltpu.SemaphoreType.DMA(())],
    )
    return pl.pallas_call(
        functools.partial(_gather_kernel, tm=tm),
        out_shape=jax.ShapeDtypeStruct((s, d), BF16),
        grid_spec=grid_spec,
        compiler_params=_params("arbitrary"),
        name="moe_gather",
    )(slot_tok, xn)


def _combine_kernel(pos_ref, y_hbm, h_ref, gate_ref, gf_ref, o_ref, buf, sem, *, tm):
    base = pl.program_id(0) * tm

    def row_copy(r, k):
        return pltpu.make_async_copy(y_hbm.at[pl.ds(pos_ref[(base + r) * TOP_K + k], 1)],
                                     buf.at[k, pl.ds(r, 1)], sem)

    def start(r, c):
        for k in range(TOP_K):
            row_copy(r, k).start()
        return c

    def wait(r, c):
        for k in range(TOP_K):
            row_copy(r, k).wait()
        return c

    lax.fori_loop(0, tm, start, 0)
    lax.fori_loop(0, tm, wait, 0)
    gate = gate_ref[...]
    y = buf[0] * gate[:, 0:1] + buf[1] * gate[:, 1:2]
    o_ref[...] = _rms(h_ref[...] + y, gf_ref[...])


def moe_combine(yb, pos, h, gate, g_final, *, tm):
    t, d = h.shape
    grid_spec = pltpu.PrefetchScalarGridSpec(
        num_scalar_prefetch=1,
        grid=(t // tm,),
        in_specs=[pl.BlockSpec(memory_space=pl.ANY),
                  pl.BlockSpec((tm, d), lambda i, pos: (i, 0)),
                  pl.BlockSpec((tm, LANES), lambda i, pos: (i, 0)),
                  pl.BlockSpec((1, d), lambda i, pos: (0, 0))],
        out_specs=pl.BlockSpec((tm, d), lambda i, pos: (i, 0)),
        scratch_shapes=[pltpu.VMEM((TOP_K, tm, d), F32), pltpu.SemaphoreType.DMA(())],
    )
    return pl.pallas_call(
        functools.partial(_combine_kernel, tm=tm),
        out_shape=jax.ShapeDtypeStruct((t, d), F32),
        grid_spec=grid_spec,
        compiler_params=_params("arbitrary"),
        name="moe_combine",
    )(pos, yb, h, gate, g_final.reshape(1, d))


def _ret_qk_kernel(x_ref, w_ref, cos_ref, sin_ref, o_ref, *, heads_per_step, dk):
    r = jnp.dot(x_ref[...], w_ref[...], preferred_element_type=F32)
    cos, sin = cos_ref[...], sin_ref[...]
    half = dk // 2
    for h in range(heads_per_step):
        e = r[:, h * dk:h * dk + half]
        o = r[:, h * dk + half:(h + 1) * dk]
        o_ref[:, h * dk:h * dk + half] = (e * cos - o * sin).astype(o_ref.dtype)
        o_ref[:, h * dk + half:(h + 1) * dk] = (o * cos + e * sin).astype(o_ref.dtype)


def ret_qk(xn, w_qk, cos_t, sin_t, *, dk):
    t, d = xn.shape
    n = w_qk.shape[1]
    tm = _tile(t, (640, 512, 256, 128, 64, 32, 16, 8))
    tn = _tile(n, (1024, 512, 256))
    return pl.pallas_call(
        functools.partial(_ret_qk_kernel, heads_per_step=tn // dk, dk=dk),
        out_shape=jax.ShapeDtypeStruct((t, n), BF16),
        grid=(n // tn, t // tm),
        in_specs=[pl.BlockSpec((tm, d), lambda j, i: (i, 0)), pl.BlockSpec((d, tn), lambda j, i: (0, j)),
                  pl.BlockSpec((tm, dk // 2), lambda j, i: (i, 0)),
                  pl.BlockSpec((tm, dk // 2), lambda j, i: (i, 0))],
        out_specs=pl.BlockSpec((tm, tn), lambda j, i: (i, j)),
        compiler_params=_params("parallel", "parallel"),
        name="ret_qk",
    )(xn, w_qk, cos_t, sin_t)


def _ret_kernel(q_ref, k_ref, v_ref, g_ref, s0_ref, dmat_ref, qdec_ref, kdec_ref, o_ref, s_out_ref, s_scr, *, nc):
    c = pl.program_id(2)

    @pl.when(c == 0)
    def _():
        s_scr[...] = s0_ref[...]

    q, k, v = q_ref[...], k_ref[...], v_ref[...]
    state = s_scr[...]
    qdec = qdec_ref[...][:, 0:1]
    kdec = kdec_ref[...][:, 0:1]
    scores = lax.dot_general(q, k, (((1,), (1,)), ((), ())), preferred_element_type=F32) * dmat_ref[...]
    intra = jnp.dot(scores.astype(BF16), v, preferred_element_type=F32)
    inter = jnp.dot(q, state.astype(BF16), preferred_element_type=F32) * qdec
    o = intra + inter
    kd = (k.astype(F32) * kdec).astype(BF16)
    chunk_decay = qdec[q.shape[0] - 1:, :]
    new_state = chunk_decay * state + lax.dot_general(kd, v, (((0,), (0,)), ((), ())), preferred_element_type=F32)
    s_scr[...] = new_state
    g = g_ref[...].astype(F32)
    o_ref[...] = (g * jax.nn.sigmoid(g) * _rms(o)).astype(o_ref.dtype)

    @pl.when(c == nc - 1)
    def _():
        s_out_ref[...] = new_state


def retention(qk, vg, s0, log_g, *, batch, heads, length, chunk, row0):
    dk, dv = s0.shape[2], s0.shape[3]
    nc = length // chunk
    blk0 = row0 // chunk
    idx = jnp.arange(chunk, dtype=F32)
    diff = idx[:, None] - idx[None, :]
    dmat = jnp.exp(jnp.where(diff[None] >= 0, diff[None] * log_g[:, None, None], -jnp.inf))
    qdec = jnp.broadcast_to(jnp.exp((idx[None, :] + 1.0) * log_g[:, None])[:, :, None], (heads, chunk, LANES))
    kdec = jnp.broadcast_to(jnp.exp((chunk - 1.0 - idx)[None, :] * log_g[:, None])[:, :, None], (heads, chunk, LANES))
    row = lambda b, h, c: blk0 + b * nc + c
    return pl.pallas_call(
        functools.partial(_ret_kernel, nc=nc),
        out_shape=(jax.ShapeDtypeStruct((batch * length, heads * dv), BF16),
                   jax.ShapeDtypeStruct((batch, heads, dk, dv), F32)),
        grid=(batch, heads, nc),
        in_specs=[pl.BlockSpec((chunk, dk), lambda b, h, c: (row(b, h, c), h)),
                  pl.BlockSpec((chunk, dk), lambda b, h, c: (row(b, h, c), heads + h)),
                  pl.BlockSpec((chunk, dv), lambda b, h, c: (row(b, h, c), h)),
                  pl.BlockSpec((chunk, dv), lambda b, h, c: (row(b, h, c), heads + h)),
                  pl.BlockSpec((None, None, dk, dv), lambda b, h, c: (b, h, 0, 0)),
                  pl.BlockSpec((None, chunk, chunk), lambda b, h, c: (h, 0, 0)),
                  pl.BlockSpec((None, chunk, LANES), lambda b, h, c: (h, 0, 0)),
                  pl.BlockSpec((None, chunk, LANES), lambda b, h, c: (h, 0, 0))],
        out_specs=(pl.BlockSpec((chunk, dv), lambda b, h, c: (b * nc + c, h)),
                   pl.BlockSpec((None, None, dk, dv), lambda b, h, c: (b, h, 0, 0))),
        scratch_shapes=[pltpu.VMEM((dk, dv), F32)],
        compiler_params=_params("parallel", "parallel", "arbitrary"),
        name="retention",
    )(qk, qk, vg, vg, s0, dmat, qdec, kdec)


def _rope_half_tables(pos):
    half = MLA_ROPE // 2
    inv = 1.0 / (MLA_ROPE_BASE ** (jnp.arange(half, dtype=F32) / half))
    ang = pos.astype(F32)[:, None] * inv[None, :]
    c, s = jnp.cos(ang), jnp.sin(ang)
    z = jnp.zeros((pos.shape[0], LANES - MLA_ROPE), F32)
    return jnp.concatenate([c, c, z], axis=-1), jnp.concatenate([s, s, z], axis=-1)


def _rope_pair_tables(pos, half):
    inv = 1.0 / (RET_ROPE_BASE ** jnp.linspace(0.0, 1.0, half, dtype=F32))
    ang = pos.astype(F32)[:, None] * inv[None, :]
    return jnp.cos(ang), jnp.sin(ang)


def _rot_half_cols(w):
    half = w.shape[-1] // 2
    return jnp.concatenate([-w[..., half:], w[..., :half]], axis=-1)


def _pad_lanes(w):
    return jnp.pad(w, [(0, 0)] * (w.ndim - 1) + [(0, LANES - w.shape[-1])])


def _even_odd_cols(w, heads, dk):
    d = w.shape[0]
    return w.reshape(d, heads, dk // 2, 2).transpose(0, 1, 3, 2).reshape(d, heads * dk)


def _even_odd_rows(s, inverse=False):
    b, h, dk, dv = s.shape
    if inverse:
        return s.reshape(b, h, 2, dk // 2, dv).transpose(0, 1, 3, 2, 4).reshape(b, h, dk, dv)
    return s.reshape(b, h, dk // 2, 2, dv).transpose(0, 1, 3, 2, 4).reshape(b, h, dk, dv)


def _route(idx, n_tok):
    n_assign = n_tok * TOP_K
    flat_e = idx.reshape(-1)
    onehot = (flat_e[:, None] == jnp.arange(N_EXPERTS, dtype=jnp.int32)[None, :]).astype(jnp.int32)
    csum = jnp.cumsum(onehot, axis=0)
    rank = jnp.sum((csum - onehot) * onehot, axis=1)
    counts = csum[-1]
    padded = (counts + MOE_BLOCK - 1) // MOE_BLOCK * MOE_BLOCK
    pad_end = jnp.cumsum(padded)
    pad_start = pad_end - padded
    dest = (jnp.sum(pad_start[None, :] * onehot, axis=1) + rank).astype(jnp.int32)
    n_blocks = (n_assign + N_EXPERTS * (MOE_BLOCK - 1)) // MOE_BLOCK
    flat_tok = jnp.arange(n_assign, dtype=jnp.int32) // TOP_K
    slot_tok = jnp.zeros((n_blocks * MOE_BLOCK,), jnp.int32).at[dest].set(flat_tok)
    block_start = jnp.arange(n_blocks, dtype=jnp.int32) * MOE_BLOCK
    block_e = jnp.minimum(jnp.sum(block_start[:, None] >= pad_end[None, :], axis=1), N_EXPERTS - 1).astype(jnp.int32)
    n_active = (pad_end[-1] // MOE_BLOCK).astype(jnp.int32).reshape(1)
    return slot_tok, dest, block_e, n_active


def kernel(x_prompt, x_sample, mem_prompt, cache_mla_ckv, cache_mla_krope, state_ret, cache_mem_k, cache_mem_v, norm_mix, norm_xattn, norm_ffn, norm_final, mla_w_dq, mla_g_q, mla_w_uq, mla_w_dkv, mla_g_kv, mla_w_uk, mla_w_uv, mla_w_o, ret_w_q, ret_w_k, ret_w_v, ret_w_g, ret_w_o, xa_w_q, xa_w_k, xa_w_v, xa_w_o, ffn_w1, ffn_w3, ffn_w2, moe_w_router, moe_w1, moe_w3, moe_w2):
    bp, lp, d = x_prompt.shape
    bs, ls, _ = x_sample.shape
    past = cache_mla_ckv.shape[2]
    n_mem = mem_prompt.shape[1]
    tp, ts = bp * lp, bs * ls
    t = tp + ts
    heads = mla_w_uk.shape[2]
    q_rank, kv_rank = mla_w_dq.shape[2], mla_w_uk.shape[1]
    dk = d // RET_HEADS
    dv = 2 * dk
    xa_dim = XA_HEADS * XA_HEAD_DIM

    pos = jnp.concatenate([jnp.tile(jnp.arange(lp, dtype=jnp.int32), bp),
                           jnp.tile(past + jnp.arange(ls, dtype=jnp.int32), bs)])
    h0 = jnp.concatenate([x_prompt.reshape(tp, d), x_sample.reshape(ts, d)], axis=0)
    tm_x = _tile(lp, (128, 64, 32))

    def cross_attention(h, layer, k_p, v_p, xn_dtype):
        wq, wo = xa_w_q[layer].astype(BF16), xa_w_o[layer].astype(BF16)
        k_all = jnp.concatenate([k_p, cache_mem_k[layer].reshape(bs * n_mem, xa_dim)], axis=0).astype(BF16)
        v_all = jnp.concatenate([v_p, cache_mem_v[layer].reshape(bs * n_mem, xa_dim)], axis=0).astype(BF16)
        return xattn_block(h, norm_xattn[layer], wq, k_all, v_all, wo, norm_ffn[layer], n_long=bp, len_long=lp,
                           len_short=ls, n_mem=n_mem, tm=tm_x, xn_dtype=xn_dtype)

    def memory_kv(layer):
        w = jnp.concatenate([xa_w_k[layer], xa_w_v[layer]], axis=1).astype(BF16)
        kv = matmul(mem_prompt.reshape(bp * n_mem, d).astype(BF16), w, out_dtype=F32, name="mem_kv")
        return kv[:, :xa_dim], kv[:, xa_dim:]

    xn = rmsnorm(h0, norm_mix[0], BF16)
    cos_m, sin_m = _rope_half_tables(pos)
    w_dkv = mla_w_dkv[0]
    w_kr = w_dkv[:, kv_rank:]
    w_pre = jnp.concatenate([mla_w_dq[0], w_dkv[:, :kv_rank], _pad_lanes(w_kr), _pad_lanes(_rot_half_cols(w_kr))],
                            axis=1).astype(BF16)
    cq, ckv, ckv_b, kr, kr_b = mla_pre(xn, w_pre, mla_g_q[0], mla_g_kv[0], cos_m, sin_m)

    w_uq = mla_w_uq[0].reshape(q_rank, heads, MLA_NOPE + MLA_ROPE)
    wn = w_uq[:, :, :MLA_NOPE].reshape(q_rank, heads * LANES).astype(BF16)
    wr = _pad_lanes(w_uq[:, :, MLA_NOPE:]).reshape(q_rank, heads * LANES).astype(BF16)
    wrot = _pad_lanes(_rot_half_cols(w_uq[:, :, MLA_NOPE:])).reshape(q_rank, heads * LANES).astype(BF16)
    q = mla_q(cq, wn, wr, wrot, cos_m, sin_m)

    w_ukv = jnp.concatenate([mla_w_uk[0].reshape(kv_rank, heads * MLA_NOPE),
                             mla_w_uv[0].reshape(kv_rank, heads * MLA_V)], axis=1).astype(BF16)
    kv = matmul(ckv_b, w_ukv, out_dtype=BF16, name="mla_kv")
    t_att = _tile(lp, (1024, 512, 256, 128, 64))
    o_p = flash_attention(q, kv, kr_b, batch=bp, heads=heads, lq=lp, lk=lp, lk_valid=lp, q_off=0,
                          tq=t_att, tk=t_att)

    lk_s = past + ls
    lk_pad = -(-lk_s // LANES) * LANES
    ckv_all = jnp.concatenate([cache_mla_ckv[0].astype(BF16), ckv_b[tp:].reshape(bs, ls, kv_rank),
                               jnp.zeros((bs, lk_pad - lk_s, kv_rank), BF16)], axis=1)
    kr_all = jnp.concatenate([_pad_lanes(cache_mla_krope[0]).astype(BF16), kr_b[tp:].reshape(bs, ls, LANES),
                              jnp.zeros((bs, lk_pad - lk_s, LANES), BF16)], axis=1)
    kv_s = matmul(ckv_all.reshape(bs * lk_pad, kv_rank), w_ukv, out_dtype=BF16, name="mla_kv_sample")
    o_s = flash_attention(q[tp:], kv_s, kr_all.reshape(bs * lk_pad, LANES), batch=bs, heads=heads, lq=ls,
                          lk=lk_pad, lk_valid=lk_s, q_off=past, tq=ls, tk=lk_pad)
    o = jnp.concatenate([o_p, o_s], axis=0)
    h1 = matmul(o, mla_w_o[0].astype(BF16), out_dtype=F32, res=h0, name="mla_out")

    mk0, mv0 = memory_kv(0)
    h2, xn_f = cross_attention(h1, 0, mk0, mv0, BF16)

    one_block = jnp.zeros((t // _tile(t, (1280, 640, 256, 128, 64, 32)),), jnp.int32)
    all_active = jnp.full((1,), one_block.shape[0], jnp.int32)
    d_ff = ffn_w1.shape[2]
    hf = expert_gateup(xn_f, ffn_w1.astype(BF16), ffn_w3.astype(BF16), one_block, all_active,
                       tm=t // one_block.shape[0], tf=_tile(d_ff, (512, 256, 128)))
    h3 = matmul(hf, ffn_w2[0].astype(BF16), out_dtype=F32, res=h2, name="ffn_down")

    xn = rmsnorm(h3, norm_mix[1], BF16)
    cos_r, sin_r = _rope_pair_tables(pos, dk // 2)
    w_qk = jnp.concatenate([_even_odd_cols(ret_w_q[0], RET_HEADS, dk),
                            _even_odd_cols(ret_w_k[0] * (dk ** -0.5), RET_HEADS, dk)], axis=1).astype(BF16)
    qk = ret_qk(xn, w_qk, cos_r, sin_r, dk=dk)
    vg = matmul(xn, jnp.concatenate([ret_w_v[0], ret_w_g[0]], axis=1).astype(BF16), out_dtype=BF16, name="ret_vg")
    log_g = jnp.log1p(-jnp.exp2(-5.0 - jnp.arange(RET_HEADS, dtype=F32)))
    ch_p = _tile(lp, (256, 128, 64))
    og_p, st_p = retention(qk, vg, jnp.zeros((bp, RET_HEADS, dk, dv), F32), log_g, batch=bp, heads=RET_HEADS,
                           length=lp, chunk=ch_p, row0=0)
    og_s, st_s = retention(qk, vg, _even_odd_rows(state_ret[0]), log_g, batch=bs, heads=RET_HEADS,
                           length=ls, chunk=ls, row0=tp)
    og = jnp.concatenate([og_p, og_s], axis=0)
    h4 = matmul(og, ret_w_o[0].astype(BF16), out_dtype=F32, res=h3, name="ret_out")

    mk1, mv1 = memory_kv(1)
    h5, xn_f = cross_attention(h4, 1, mk1, mv1, F32)

    idx_l, gate_l = moe_router(xn_f, moe_w_router[0])
    slot_tok, dest, block_e, n_active = _route(idx_l[:, :TOP_K], t)
    xs = moe_gather(xn_f, slot_tok, tm=MOE_BLOCK)
    hs = expert_gateup(xs, moe_w1[0].astype(BF16), moe_w3[0].astype(BF16), block_e, n_active,
                       tm=MOE_BLOCK, tf=_tile(d_ff, (1024, 512, 256, 128)))
    yb = expert_down(hs, moe_w2[0].astype(BF16), block_e, n_active, tm=MOE_BLOCK, tn=_tile(d, (512, 256, 128)))
    y = moe_combine(yb, dest, h5, gate_l, norm_final, tm=_tile(t, (128, 64, 32, 16, 8)))

    def split(a, width):
        return a[:tp].reshape(1, bp, lp, width), a[tp:].reshape(1, bs, ls, width)

    ckv_p, ckv_s = split(ckv, kv_rank)
    kr_p, kr_s = split(kr, MLA_ROPE)
    mem_shape = (bp, n_mem, XA_HEADS, XA_HEAD_DIM)
    return (y[:tp].reshape(bp, lp, d), y[tp:].reshape(bs, ls, d),
            ckv_p, kr_p, _even_odd_rows(st_p, inverse=True)[None],
            jnp.stack([mk0.reshape(mem_shape), mk1.reshape(mem_shape)]),
            jnp.stack([mv0.reshape(mem_shape), mv1.reshape(mem_shape)]),
            ckv_s, kr_s, _even_odd_rows(st_s, inverse=True)[None])
```

```python
import functools

import jax
import jax.numpy as jnp
from jax import lax
from jax.experimental import pallas as pl
from jax.experimental.pallas import tpu as pltpu

F32 = jnp.float32
BF16 = jnp.bfloat16

RMS_EPS = 1e-6
CHUNK = 64
MLA_NOPE = 128
MLA_ROPE = 64
MLA_V = 128
MLA_ROPE_BASE = 10000.0
RET_HEADS = 16
RET_ROPE_BASE = 10000.0
XA_HEADS = 4
XA_HEAD_DIM = 128
N_EXPERTS = 8
TOP_K = 2
MOE_BLOCK = 256
LANES = 128
NEG_BIG = -1e30
VMEM_LIMIT_BYTES = 56 * 1024 * 1024


def _params(*sem):
    return pltpu.CompilerParams(dimension_semantics=sem, vmem_limit_bytes=VMEM_LIMIT_BYTES)


def _tile(n, candidates):
    for c in candidates:
        if c <= n and n % c == 0:
            return c
    return n


def _rms(x, g=None):
    y = x * lax.rsqrt(jnp.mean(x * x, axis=-1, keepdims=True) + RMS_EPS)
    return y if g is None else y * g


def _rmsnorm_kernel(x_ref, g_ref, o_ref):
    o_ref[...] = _rms(x_ref[...], g_ref[...]).astype(o_ref.dtype)


def rmsnorm(x, g, out_dtype):
    t, d = x.shape
    tm = _tile(t, (256, 128, 64, 32, 16, 8))
    return pl.pallas_call(
        _rmsnorm_kernel,
        out_shape=jax.ShapeDtypeStruct((t, d), out_dtype),
        grid=(t // tm,),
        in_specs=[pl.BlockSpec((tm, d), lambda i: (i, 0)), pl.BlockSpec((1, d), lambda i: (0, 0))],
        out_specs=pl.BlockSpec((tm, d), lambda i: (i, 0)),
        compiler_params=_params("parallel"),
        name="rmsnorm",
    )(x, g.reshape(1, d))


def _mm_kernel(*refs, nk, has_res):
    a_ref, w_ref = refs[0], refs[1]
    res_ref = refs[2] if has_res else None
    o_ref = refs[2 + has_res]
    part = jnp.dot(a_ref[...], w_ref[...], preferred_element_type=F32)
    if nk == 1:
        if has_res:
            part = part + res_ref[...]
        o_ref[...] = part.astype(o_ref.dtype)
        return
    acc_ref = refs[3 + has_res]
    k = pl.program_id(2)

    @pl.when(k == 0)
    def _():
        acc_ref[...] = part

    @pl.when(k > 0)
    def _():
        acc_ref[...] += part

    @pl.when(k == nk - 1)
    def _():
        out = acc_ref[...]
        if has_res:
            out = out + res_ref[...]
        o_ref[...] = out.astype(o_ref.dtype)


def matmul(a, w, *, out_dtype, res=None, tm=None, tn=None, tk=None, name="matmul"):
    m, kdim = a.shape
    n = w.shape[1]
    tk = tk or (kdim if kdim <= 4096 else _tile(kdim, (2048, 1024, 512)))
    row_tiles = (1280, 1024) if tk <= 2048 else ()
    tm = tm or _tile(m, row_tiles + (640, 512, 256, 128, 64, 32, 16, 8))
    tn = tn or _tile(n, (1024, 512, 256, 128))
    nk = kdim // tk
    has_res = res is not None
    in_specs = [pl.BlockSpec((tm, tk), lambda j, i, k: (i, k)),
                pl.BlockSpec((tk, tn), lambda j, i, k: (k, j))]
    args = [a, w]
    if has_res:
        in_specs.append(pl.BlockSpec((tm, tn), lambda j, i, k: (i, j)))
        args.append(res)
    scratch = [pltpu.VMEM((tm, tn), F32)] if nk > 1 else []
    return pl.pallas_call(
        functools.partial(_mm_kernel, nk=nk, has_res=has_res),
        out_shape=jax.ShapeDtypeStruct((m, n), out_dtype),
        grid=(n // tn, m // tm, nk),
        in_specs=in_specs,
        out_specs=pl.BlockSpec((tm, tn), lambda j, i, k: (i, j)),
        scratch_shapes=scratch,
        compiler_params=_params("parallel", "parallel", "arbitrary"),
        name=name,
    )(*args)


def _mla_pre_kernel(x_ref, w_ref, gq_ref, gkv_ref, cos_ref, sin_ref,
                    cq_ref, ckv_ref, ckvb_ref, kr_ref, krb_ref, *, q_rank, kv_rank):
    r = jnp.dot(x_ref[...], w_ref[...], preferred_element_type=F32)
    cq_ref[...] = _rms(r[:, :q_rank], gq_ref[...]).astype(cq_ref.dtype)
    ckv = _rms(r[:, q_rank:q_rank + kv_rank], gkv_ref[...])
    ckv_ref[...] = ckv
    ckvb_ref[...] = ckv.astype(ckvb_ref.dtype)
    o = q_rank + kv_rank
    kr = r[:, o:o + LANES] * cos_ref[...] + r[:, o + LANES:o + 2 * LANES] * sin_ref[...]
    kr_ref[...] = kr[:, :MLA_ROPE]
    krb_ref[...] = kr.astype(krb_ref.dtype)


def mla_pre(xn, w_pre, g_q, g_kv, cos_t, sin_t):
    t, d = xn.shape
    q_rank, kv_rank = g_q.shape[0], g_kv.shape[0]
    n = w_pre.shape[1]
    tm = _tile(t, (256, 128, 64, 32, 16, 8))
    row = lambda i: (i, 0)
    fixed = lambda i: (0, 0)
    return pl.pallas_call(
        functools.partial(_mla_pre_kernel, q_rank=q_rank, kv_rank=kv_rank),
        out_shape=(jax.ShapeDtypeStruct((t, q_rank), BF16),
                   jax.ShapeDtypeStruct((t, kv_rank), F32),
                   jax.ShapeDtypeStruct((t, kv_rank), BF16),
                   jax.ShapeDtypeStruct((t, MLA_ROPE), F32),
                   jax.ShapeDtypeStruct((t, LANES), BF16)),
        grid=(t // tm,),
        in_specs=[pl.BlockSpec((tm, d), row), pl.BlockSpec((d, n), fixed),
                  pl.BlockSpec((1, q_rank), fixed), pl.BlockSpec((1, kv_rank), fixed),
                  pl.BlockSpec((tm, LANES), row), pl.BlockSpec((tm, LANES), row)],
        out_specs=(pl.BlockSpec((tm, q_rank), row), pl.BlockSpec((tm, kv_rank), row),
                   pl.BlockSpec((tm, kv_rank), row), pl.BlockSpec((tm, MLA_ROPE), row),
                   pl.BlockSpec((tm, LANES), row)),
        compiler_params=_params("parallel"),
        name="mla_pre",
    )(xn, w_pre, g_q.reshape(1, -1), g_kv.reshape(1, -1), cos_t, sin_t)


def _mla_q_kernel(c_ref, wn_ref, wr_ref, cs_ref, q_ref, *, heads_per_step, scale):
    c = c_ref[...]
    qn = jnp.dot(c, wn_ref[...], preferred_element_type=F32) * scale
    qr = jnp.dot(c, wr_ref[...], preferred_element_type=F32)
    cs = cs_ref[...] * scale
    for h in range(heads_per_step):
        lo, hi = h * LANES, (h + 1) * LANES
        q_ref[:, 2 * lo:2 * lo + LANES] = qn[:, lo:hi].astype(q_ref.dtype)
        terms = qr[:, lo:hi] * cs
        q_ref[:, 2 * lo + LANES:2 * hi] = (terms + pltpu.roll(terms, LANES // 2, 1)).astype(q_ref.dtype)


def mla_q(cq, wn, wr, cs_t, *, scale):
    t, q_rank = cq.shape
    n = wn.shape[1]
    tm = _tile(t, (1280, 1024, 640, 512, 256, 128, 64, 32, 16, 8))
    tn = _tile(n, (512, 256, 128))
    w_spec = pl.BlockSpec((q_rank, tn), lambda j, i: (0, j))
    return pl.pallas_call(
        functools.partial(_mla_q_kernel, heads_per_step=tn // LANES, scale=scale),
        out_shape=jax.ShapeDtypeStruct((t, 2 * n), BF16),
        grid=(n // tn, t // tm),
        in_specs=[pl.BlockSpec((tm, q_rank), lambda j, i: (i, 0)), w_spec, w_spec,
                  pl.BlockSpec((tm, LANES), lambda j, i: (i, 0))],
        out_specs=pl.BlockSpec((tm, 2 * tn), lambda j, i: (i, j)),
        compiler_params=_params("parallel", "parallel"),
        name="mla_q",
    )(cq, wn, wr, cs_t)


def _flash_last_k(qi, *, tq, tk, nk, q_off):
    last_pos = ((q_off + (qi + 1) * tq - 1) // CHUNK) * CHUNK + CHUNK - 1
    return min(nk - 1, last_pos // tk)


def _flash_kernel(qi_ref, ki_ref, q_ref, kn_ref, kr_ref, v_ref, o_ref, m_scr, l_scr, acc_scr,
                  *, tq, tk, rb, hp, nk, q_off, lk_valid, aligned):
    step = pl.program_id(2)
    qi, ki = qi_ref[step], ki_ref[step]
    first_q = q_off + qi * tq
    last_seen_by_first_row = (first_q // CHUNK) * CHUNK + CHUNK - 1
    tile_last_key = ki * tk + tk - 1
    needs_mask = jnp.logical_or(tile_last_key > last_seen_by_first_row, tile_last_key >= lk_valid)
    last_ki = jnp.minimum(nk - 1, (((first_q + tq - 1) // CHUNK) * CHUNK + CHUNK - 1) // tk)
    shift = CHUNK.bit_length() - 1

    @pl.when(ki == 0)
    def _():
        m_scr[...] = jnp.full(m_scr.shape, NEG_BIG, F32)
        l_scr[...] = jnp.zeros(l_scr.shape, F32)
        acc_scr[...] = jnp.zeros(acc_scr.shape, F32)

    def attend(masked):
        block = rb if (masked and aligned) else tq
        for h in range(hp):
            hl = slice(h * LANES, (h + 1) * LANES)
            for r in range(tq // block):
                rows = slice(r * block, (r + 1) * block)
                nc = min(tk, (r + 1) * block) if (masked and aligned) else tk
                k = jnp.concatenate([kn_ref[:nc, hl], kr_ref[:nc, :]], axis=-1)
                s = lax.dot_general(q_ref[rows, 2 * h * LANES:2 * (h + 1) * LANES], k, (((1,), (1,)), ((), ())),
                                    preferred_element_type=F32)
                if masked:
                    qpos = first_q + r * block + lax.broadcasted_iota(jnp.int32, (block, nc), 0)
                    kpos = ki * tk + lax.broadcasted_iota(jnp.int32, (block, nc), 1)
                    s = jnp.where(lax.shift_right_arithmetic(kpos, shift) <= lax.shift_right_arithmetic(qpos, shift),
                                  s, NEG_BIG)
                    if lk_valid < nk * tk:
                        s = jnp.where(kpos < lk_valid, s, NEG_BIG)
                m_prev = m_scr[h, rows, :]
                m_new = jnp.maximum(m_prev, jnp.max(s, axis=-1, keepdims=True))
                alpha = jnp.exp2(m_prev - m_new)
                p = jnp.exp2(s - jnp.tile(m_new, (1, nc // LANES)))
                l_scr[h, rows, :] = alpha * l_scr[h, rows, :] + jnp.sum(p, axis=-1, keepdims=True)
                acc_scr[rows, hl] = alpha * acc_scr[rows, hl] + jnp.dot(p.astype(BF16), v_ref[:nc, hl],
                                                                       preferred_element_type=F32)
                m_scr[h, rows, :] = m_new

    pl.when(needs_mask)(functools.partial(attend, True))
    pl.when(jnp.logical_not(needs_mask))(functools.partial(attend, False))

    @pl.when(ki == last_ki)
    def _():
        for h in range(hp):
            hl = slice(h * LANES, (h + 1) * LANES)
            o_ref[:, hl] = (acc_scr[:, hl] / l_scr[h]).astype(o_ref.dtype)


def flash_attention(q, kv, kr, *, batch, heads, lq, lk, lk_valid, q_off, tq, tk):
    assert CHUNK & (CHUNK - 1) == 0 and MLA_V == LANES and MLA_NOPE == LANES
    nq, nk = lq // tq, lk // tk
    pairs = [(qi, ki) for qi in range(nq)
             for ki in range(_flash_last_k(qi, tq=tq, tk=tk, nk=nk, q_off=q_off) + 1)]
    qi_of = jnp.asarray([p[0] for p in pairs], jnp.int32)
    ki_of = jnp.asarray([p[1] for p in pairs], jnp.int32)
    hp = 2 if heads % 2 == 0 else 1
    hg = heads // hp
    rb = _tile(tq, (256, 128, 64, 32, 16))
    aligned = q_off % tq == 0 and tq == tk and lk_valid == lk and rb % max(CHUNK, LANES) == 0
    grid_spec = pltpu.PrefetchScalarGridSpec(
        num_scalar_prefetch=2,
        grid=(batch, hg, len(pairs)),
        in_specs=[pl.BlockSpec((tq, 2 * hp * LANES), lambda b, h, s, qi, ki: (b * nq + qi[s], h)),
                  pl.BlockSpec((tk, hp * LANES), lambda b, h, s, qi, ki: (b * nk + ki[s], h)),
                  pl.BlockSpec((tk, LANES), lambda b, h, s, qi, ki: (b * nk + ki[s], 0)),
                  pl.BlockSpec((tk, hp * LANES), lambda b, h, s, qi, ki: (b * nk + ki[s], hg + h))],
        out_specs=pl.BlockSpec((tq, hp * LANES), lambda b, h, s, qi, ki: (b * nq + qi[s], h)),
        scratch_shapes=[pltpu.VMEM((hp, tq, LANES), F32), pltpu.VMEM((hp, tq, LANES), F32),
                        pltpu.VMEM((tq, hp * LANES), F32)],
    )
    return pl.pallas_call(
        functools.partial(_flash_kernel, tq=tq, tk=tk, rb=rb, hp=hp, nk=nk, q_off=q_off, lk_valid=lk_valid,
                          aligned=aligned),
        out_shape=jax.ShapeDtypeStruct((batch * lq, heads * MLA_V), BF16),
        grid_spec=grid_spec,
        compiler_params=_params("parallel", "parallel", "arbitrary"),
        name="mla_flash",
    )(qi_of, ki_of, q, kv, kr, kv)


def _xattn_attend(q, k, v):
    scale = float(XA_HEAD_DIM ** -0.5)
    outs = []
    for hd in range(XA_HEADS):
        sl = slice(hd * XA_HEAD_DIM, (hd + 1) * XA_HEAD_DIM)
        s = lax.dot_general(q[:, sl], k[:, sl], (((1,), (1,)), ((), ())), preferred_element_type=F32) * scale
        p = jnp.exp(s - jnp.max(s, axis=-1, keepdims=True))
        p = p / jnp.sum(p, axis=-1, keepdims=True)
        outs.append(jnp.dot(p.astype(BF16), v[:, sl], preferred_element_type=F32).astype(BF16))
    return jnp.concatenate(outs, axis=-1)


def _xattn_kernel(*refs, sub, n_long_tiles):
    h_ref, gx_ref, wq_ref, wo_ref, gf_ref = refs[:5]
    k_refs, v_refs = refs[5:5 + sub], refs[5 + sub:5 + 2 * sub]
    h_out, xn_out, o_scr = refs[5 + 2 * sub:]
    i = pl.program_id(0)
    x = h_ref[...]
    xn = _rms(x, gx_ref[...]).astype(BF16)
    q = jnp.dot(xn, wq_ref[...], preferred_element_type=F32).astype(BF16)
    rows = q.shape[0] // sub

    @pl.when(i < n_long_tiles)
    def _():
        o_scr[...] = _xattn_attend(q, k_refs[0][...], v_refs[0][...])

    @pl.when(i >= n_long_tiles)
    def _():
        for j in range(sub):
            o_scr[j * rows:(j + 1) * rows, :] = _xattn_attend(q[j * rows:(j + 1) * rows], k_refs[j][...], v_refs[j][...])

    h_new = x + jnp.dot(o_scr[...], wo_ref[...], preferred_element_type=F32)
    h_out[...] = h_new
    xn_out[...] = _rms(h_new, gf_ref[...]).astype(xn_out.dtype)


def xattn_block(h, g_x, w_q, k, v, w_o, g_f, *, n_long, len_long, len_short, n_mem, tm, xn_dtype):
    t, d = h.shape
    hd = XA_HEADS * XA_HEAD_DIM
    sub = tm // len_short
    tiles_per_long = len_long // tm
    n_long_tiles = n_long * tiles_per_long
    row = lambda i: (i, 0)
    fixed = lambda i: (0, 0)

    def mem(j):
        def index(i):
            short = n_long + (i - n_long_tiles) * sub + j
            return (jnp.where(i < n_long_tiles, i // tiles_per_long, short), 0)
        return pl.BlockSpec((n_mem, hd), index)

    in_specs = ([pl.BlockSpec((tm, d), row), pl.BlockSpec((1, d), fixed), pl.BlockSpec((d, hd), fixed),
                 pl.BlockSpec((hd, d), fixed), pl.BlockSpec((1, d), fixed)]
                + [mem(j) for j in range(sub)] * 2)
    return pl.pallas_call(
        functools.partial(_xattn_kernel, sub=sub, n_long_tiles=n_long_tiles),
        out_shape=(jax.ShapeDtypeStruct((t, d), F32), jax.ShapeDtypeStruct((t, d), xn_dtype)),
        grid=(t // tm,),
        in_specs=in_specs,
        out_specs=(pl.BlockSpec((tm, d), row), pl.BlockSpec((tm, d), row)),
        scratch_shapes=[pltpu.VMEM((tm, hd), BF16)],
        compiler_params=_params("parallel"),
        name="xattn",
    )(h, g_x.reshape(1, d), w_q, w_o, g_f.reshape(1, d), *([k] * sub), *([v] * sub))


def _gateup_kernel(be_ref, na_ref, x_ref, w1_ref, w3_ref, o_ref, *scratch):
    i = pl.program_id(1)
    if scratch:
        w1_scr, w3_scr = scratch

        @pl.when(jnp.logical_or(i == 0, be_ref[i] != be_ref[jnp.maximum(i - 1, 0)]))
        def _():
            w1_scr[...] = w1_ref[...].astype(BF16)
            w3_scr[...] = w3_ref[...].astype(BF16)
    else:
        w1_scr, w3_scr = w1_ref, w3_ref

    @pl.when(i < na_ref[0])
    def _():
        x = x_ref[...]
        a = jnp.dot(x, w1_scr[...], preferred_element_type=F32)
        b = jnp.dot(x, w3_scr[...], preferred_element_type=F32)
        o_ref[...] = (a * jax.nn.sigmoid(a) * b).astype(o_ref.dtype)

    @pl.when(i >= na_ref[0])
    def _():
        o_ref[...] = jnp.zeros(o_ref.shape, o_ref.dtype)


def expert_gateup(xs, w1, w3, block_e, n_active, *, tm, tf):
    s, d = xs.shape
    f = w1.shape[2]
    scratch = [] if w1.dtype == BF16 else [pltpu.VMEM((d, tf), BF16), pltpu.VMEM((d, tf), BF16)]
    w_spec = pl.BlockSpec((None, d, tf), lambda j, i, be, na: (be[i], 0, j))
    grid_spec = pltpu.PrefetchScalarGridSpec(
        num_scalar_prefetch=2,
        grid=(f // tf, s // tm),
        in_specs=[pl.BlockSpec((tm, d), lambda j, i, be, na: (i, 0)), w_spec, w_spec],
        out_specs=pl.BlockSpec((tm, tf), lambda j, i, be, na: (i, j)),
        scratch_shapes=scratch,
    )
    return pl.pallas_call(
        _gateup_kernel,
        out_shape=jax.ShapeDtypeStruct((s, f), BF16),
        grid_spec=grid_spec,
        compiler_params=_params("arbitrary", "arbitrary"),
        name="expert_gateup",
    )(block_e, n_active, xs, w1, w3)


def _down_kernel(be_ref, na_ref, h_ref, w_ref, o_ref):
    i = pl.program_id(1)

    @pl.when(i < na_ref[0])
    def _():
        o_ref[...] = jnp.dot(h_ref[...], w_ref[...], preferred_element_type=F32)

    @pl.when(i >= na_ref[0])
    def _():
        o_ref[...] = jnp.zeros(o_ref.shape, o_ref.dtype)


def expert_down(hs, w2, block_e, n_active, *, tm, tn):
    s, f = hs.shape
    d = w2.shape[2]
    grid_spec = pltpu.PrefetchScalarGridSpec(
        num_scalar_prefetch=2,
        grid=(d // tn, s // tm),
        in_specs=[pl.BlockSpec((tm, f), lambda j, i, be, na: (i, 0)),
                  pl.BlockSpec((None, f, tn), lambda j, i, be, na: (be[i], 0, j))],
        out_specs=pl.BlockSpec((tm, tn), lambda j, i, be, na: (i, j)),
    )
    return pl.pallas_call(
        _down_kernel,
        out_shape=jax.ShapeDtypeStruct((s, d), F32),
        grid_spec=grid_spec,
        compiler_params=_params("parallel", "arbitrary"),
        name="expert_down",
    )(block_e, n_active, hs, w2)


def _router_kernel(x_ref, wh_ref, wl_ref, idx_ref, gate_ref):
    x = x_ref[...]
    xh = x.astype(BF16)
    xl = (x - xh.astype(F32)).astype(BF16)
    wh, wl = wh_ref[...], wl_ref[...]
    logits = (jnp.dot(xh, wh, preferred_element_type=F32) + jnp.dot(xl, wh, preferred_element_type=F32)
              + jnp.dot(xh, wl, preferred_element_type=F32))
    lane = lax.broadcasted_iota(jnp.int32, logits.shape, 1)
    logits = jnp.where(lane < N_EXPERTS, logits, -jnp.inf)
    m1 = jnp.max(logits, axis=-1, keepdims=True)
    i1 = jnp.min(jnp.where(logits == m1, lane, LANES), axis=-1, keepdims=True)
    rest = jnp.where(lane == i1, -jnp.inf, logits)
    m2 = jnp.max(rest, axis=-1, keepdims=True)
    i2 = jnp.min(jnp.where(rest == m2, lane, LANES), axis=-1, keepdims=True)
    e = jnp.exp(m2 - m1)
    g1 = 1.0 / (1.0 + e)
    g2 = e / (1.0 + e)
    idx_ref[...] = jnp.where(lane == 0, i1, jnp.where(lane == 1, i2, 0))
    gate_ref[...] = jnp.where(lane == 0, g1, jnp.where(lane == 1, g2, 0.0))


def moe_router(xn, w_router):
    t, d = xn.shape
    w = jnp.zeros((d, LANES), F32).at[:, :N_EXPERTS].set(w_router)
    wh = w.astype(BF16)
    wl = (w - wh.astype(F32)).astype(BF16)
    tm = _tile(t, (256, 128, 64, 32, 16, 8))
    row = lambda i: (i, 0)
    fixed = lambda i: (0, 0)
    return pl.pallas_call(
        _router_kernel,
        out_shape=(jax.ShapeDtypeStruct((t, LANES), jnp.int32), jax.ShapeDtypeStruct((t, LANES), F32)),
        grid=(t // tm,),
        in_specs=[pl.BlockSpec((tm, d), row), pl.BlockSpec((d, LANES), fixed), pl.BlockSpec((d, LANES), fixed)],
        out_specs=(pl.BlockSpec((tm, LANES), row), pl.BlockSpec((tm, LANES), row)),
        compiler_params=_params("parallel"),
        name="moe_router",
    )(xn, wh, wl)


def _gather_kernel(tok_ref, x_hbm, o_ref, buf, sem, *, tm):
    base = pl.program_id(0) * tm

    def row_copy(r):
        return pltpu.make_async_copy(x_hbm.at[pl.ds(tok_ref[base + r], 1)], buf.at[pl.ds(r, 1)], sem)

    def start(r, c):
        row_copy(r).start()
        return c

    def wait(r, c):
        row_copy(r).wait()
        return c

    lax.fori_loop(0, tm, start, 0)
    lax.fori_loop(0, tm, wait, 0)
    o_ref[...] = buf[...].astype(o_ref.dtype)


def moe_gather(xn, slot_tok, *, tm):
    s = slot_tok.shape[0]
    d = xn.shape[1]
    grid_spec = pltpu.PrefetchScalarGridSpec(
        num_scalar_prefetch=1,
        grid=(s // tm,),
        in_specs=[pl.BlockSpec(memory_space=pl.ANY)],
        out_specs=pl.BlockSpec((tm, d), lambda i, tok: (i, 0)),
        scratch_shapes=[pltpu.VMEM((tm, d), F32), pltpu.SemaphoreType.DMA(())],
    )
    return pl.pallas_call(
        functools.partial(_gather_kernel, tm=tm),
        out_shape=jax.ShapeDtypeStruct((s, d), BF16),
        grid_spec=grid_spec,
        compiler_params=_params("arbitrary"),
        name="moe_gather",
    )(slot_tok, xn)


def _combine_kernel(pos_ref, y_hbm, h_ref, gate_ref, gf_ref, o_first_ref, o_rest_ref, buf, sem, *, tm, n_first):
    i = pl.program_id(0)
    base = i * tm

    def row_copy(r, k):
        return pltpu.make_async_copy(y_hbm.at[pl.ds(pos_ref[(base + r) * TOP_K + k], 1)],
                                     buf.at[k, pl.ds(r, 1)], sem)

    def start(r, c):
        for k in range(TOP_K):
            row_copy(r, k).start()
        return c

    def wait(r, c):
        for k in range(TOP_K):
            row_copy(r, k).wait()
        return c

    lax.fori_loop(0, tm, start, 0)
    lax.fori_loop(0, tm, wait, 0)
    gate = gate_ref[...]
    y = buf[0] * gate[:, 0:1] + buf[1] * gate[:, 1:2]
    out = _rms(h_ref[...] + y, gf_ref[...])

    @pl.when(i < n_first)
    def _():
        o_first_ref[...] = out

    @pl.when(i >= n_first)
    def _():
        o_rest_ref[...] = out


def moe_combine(yb, pos, h, gate, g_final, *, tm, t_first):
    t, d = h.shape
    n_first = t_first // tm
    grid_spec = pltpu.PrefetchScalarGridSpec(
        num_scalar_prefetch=1,
        grid=(t // tm,),
        in_specs=[pl.BlockSpec(memory_space=pl.ANY),
                  pl.BlockSpec((tm, d), lambda i, pos: (i, 0)),
                  pl.BlockSpec((tm, LANES), lambda i, pos: (i, 0)),
                  pl.BlockSpec((1, d), lambda i, pos: (0, 0))],
        out_specs=(pl.BlockSpec((tm, d), lambda i, pos: (jnp.minimum(i, n_first - 1), 0)),
                   pl.BlockSpec((tm, d), lambda i, pos: (jnp.maximum(i - n_first, 0), 0))),
        scratch_shapes=[pltpu.VMEM((TOP_K, tm, d), F32), pltpu.SemaphoreType.DMA(())],
    )
    return pl.pallas_call(
        functools.partial(_combine_kernel, tm=tm, n_first=n_first),
        out_shape=(jax.ShapeDtypeStruct((t_first, d), F32), jax.ShapeDtypeStruct((t - t_first, d), F32)),
        grid_spec=grid_spec,
        compiler_params=_params("arbitrary"),
        name="moe_combine",
    )(pos, yb, h, gate, g_final.reshape(1, d))


def _ret_qk_kernel(x_ref, w_ref, cos_ref, sin_ref, o_ref, *, heads_per_step, dk):
    r = jnp.dot(x_ref[...], w_ref[...], preferred_element_type=F32)
    cos, sin = cos_ref[...], sin_ref[...]
    half = dk // 2
    for h in range(heads_per_step):
        e = r[:, h * dk:h * dk + half]
        o = r[:, h * dk + half:(h + 1) * dk]
        o_ref[:, h * dk:h * dk + half] = (e * cos - o * sin).astype(o_ref.dtype)
        o_ref[:, h * dk + half:(h + 1) * dk] = (o * cos + e * sin).astype(o_ref.dtype)


def ret_qk(xn, w_qk, cos_t, sin_t, *, dk):
    t, d = xn.shape
    n = w_qk.shape[1]
    tm = _tile(t, (640, 512, 256, 128, 64, 32, 16, 8))
    tn = _tile(n, (1024, 512, 256))
    return pl.pallas_call(
        functools.partial(_ret_qk_kernel, heads_per_step=tn // dk, dk=dk),
        out_shape=jax.ShapeDtypeStruct((t, n), BF16),
        grid=(n // tn, t // tm),
        in_specs=[pl.BlockSpec((tm, d), lambda j, i: (i, 0)), pl.BlockSpec((d, tn), lambda j, i: (0, j)),
                  pl.BlockSpec((tm, dk // 2), lambda j, i: (i, 0)),
                  pl.BlockSpec((tm, dk // 2), lambda j, i: (i, 0))],
        out_specs=pl.BlockSpec((tm, tn), lambda j, i: (i, j)),
        compiler_params=_params("parallel", "parallel"),
        name="ret_qk",
    )(xn, w_qk, cos_t, sin_t)


def _ret_kernel(q_ref, k_ref, v_ref, g_ref, s0_ref, dmat_ref, qdec_ref, kdec_ref, o_ref, s_out_ref, s_scr, *, nc):
    c = pl.program_id(2)

    @pl.when(c == 0)
    def _():
        s_scr[...] = s0_ref[...]

    q, k, v = q_ref[...], k_ref[...], v_ref[...]
    state = s_scr[...]
    qdec = qdec_ref[...][:, 0:1]
    kdec = kdec_ref[...][:, 0:1]
    scores = lax.dot_general(q, k, (((1,), (1,)), ((), ())), preferred_element_type=F32) * dmat_ref[...]
    intra = jnp.dot(scores.astype(BF16), v, preferred_element_type=F32)
    inter = jnp.dot(q, state.astype(BF16), preferred_element_type=F32) * qdec
    o = intra + inter
    kd = (k.astype(F32) * kdec).astype(BF16)
    chunk_decay = qdec[q.shape[0] - 1:, :]
    new_state = chunk_decay * state + lax.dot_general(kd, v, (((0,), (0,)), ((), ())), preferred_element_type=F32)
    s_scr[...] = new_state
    g = g_ref[...].astype(F32)
    o_ref[...] = (g * jax.nn.sigmoid(g) * _rms(o)).astype(o_ref.dtype)

    @pl.when(c == nc - 1)
    def _():
        s_out_ref[...] = new_state


def retention(qk, v, g, s0, log_g, *, batch, heads, length, chunk, row0):
    dk, dv = s0.shape[2], s0.shape[3]
    nc = length // chunk
    blk0 = row0 // chunk
    idx = jnp.arange(chunk, dtype=F32)
    diff = idx[:, None] - idx[None, :]
    dmat = jnp.exp(jnp.where(diff[None] >= 0, diff[None] * log_g[:, None, None], -jnp.inf))
    qdec = jnp.broadcast_to(jnp.exp((idx[None, :] + 1.0) * log_g[:, None])[:, :, None], (heads, chunk, LANES))
    kdec = jnp.broadcast_to(jnp.exp((chunk - 1.0 - idx)[None, :] * log_g[:, None])[:, :, None], (heads, chunk, LANES))
    row = lambda b, h, c: blk0 + b * nc + c
    return pl.pallas_call(
        functools.partial(_ret_kernel, nc=nc),
        out_shape=(jax.ShapeDtypeStruct((batch * length, heads * dv), BF16),
                   jax.ShapeDtypeStruct((batch, heads, dk, dv), F32)),
        grid=(batch, heads, nc),
        in_specs=[pl.BlockSpec((chunk, dk), lambda b, h, c: (row(b, h, c), h)),
                  pl.BlockSpec((chunk, dk), lambda b, h, c: (row(b, h, c), heads + h)),
                  pl.BlockSpec((chunk, dv), lambda b, h, c: (row(b, h, c), h)),
                  pl.BlockSpec((chunk, dv), lambda b, h, c: (row(b, h, c), h)),
                  pl.BlockSpec((None, None, dk, dv), lambda b, h, c: (b, h, 0, 0)),
                  pl.BlockSpec((None, chunk, chunk), lambda b, h, c: (h, 0, 0)),
                  pl.BlockSpec((None, chunk, LANES), lambda b, h, c: (h, 0, 0)),
                  pl.BlockSpec((None, chunk, LANES), lambda b, h, c: (h, 0, 0))],
        out_specs=(pl.BlockSpec((chunk, dv), lambda b, h, c: (b * nc + c, h)),
                   pl.BlockSpec((None, None, dk, dv), lambda b, h, c: (b, h, 0, 0))),
        scratch_shapes=[pltpu.VMEM((dk, dv), F32)],
        compiler_params=_params("parallel", "parallel", "arbitrary"),
        name="retention",
    )(qk, qk, v, g, s0, dmat, qdec, kdec)


def _rope_half_tables(pos):
    half = MLA_ROPE // 2
    inv = 1.0 / (MLA_ROPE_BASE ** (jnp.arange(half, dtype=F32) / half))
    ang = pos.astype(F32)[:, None] * inv[None, :]
    c, s = jnp.cos(ang), jnp.sin(ang)
    z = jnp.zeros((pos.shape[0], LANES - MLA_ROPE), F32)
    return jnp.concatenate([c, c, z], axis=-1), jnp.concatenate([s, s, z], axis=-1)


def _rope_pair_tables(pos, half):
    inv = 1.0 / (RET_ROPE_BASE ** jnp.linspace(0.0, 1.0, half, dtype=F32))
    ang = pos.astype(F32)[:, None] * inv[None, :]
    return jnp.cos(ang), jnp.sin(ang)


def _rot_half_cols(w):
    half = w.shape[-1] // 2
    return jnp.concatenate([-w[..., half:], w[..., :half]], axis=-1)


def _pad_lanes(w):
    return jnp.pad(w, [(0, 0)] * (w.ndim - 1) + [(0, LANES - w.shape[-1])])


def _even_odd_cols(w, heads, dk):
    d = w.shape[0]
    return w.reshape(d, heads, dk // 2, 2).transpose(0, 1, 3, 2).reshape(d, heads * dk)


def _even_odd_rows(s, inverse=False):
    b, h, dk, dv = s.shape
    if inverse:
        return s.reshape(b, h, 2, dk // 2, dv).transpose(0, 1, 3, 2, 4).reshape(b, h, dk, dv)
    return s.reshape(b, h, dk // 2, 2, dv).transpose(0, 1, 3, 2, 4).reshape(b, h, dk, dv)


def _route(idx, n_tok):
    n_assign = n_tok * TOP_K
    flat_e = idx.reshape(-1)
    onehot = (flat_e[:, None] == jnp.arange(N_EXPERTS, dtype=jnp.int32)[None, :]).astype(jnp.int32)
    csum = jnp.cumsum(onehot, axis=0)
    rank = jnp.sum((csum - onehot) * onehot, axis=1)
    counts = csum[-1]
    padded = (counts + MOE_BLOCK - 1) // MOE_BLOCK * MOE_BLOCK
    pad_end = jnp.cumsum(padded)
    pad_start = pad_end - padded
    dest = (jnp.sum(pad_start[None, :] * onehot, axis=1) + rank).astype(jnp.int32)
    n_blocks = (n_assign + N_EXPERTS * (MOE_BLOCK - 1)) // MOE_BLOCK
    flat_tok = jnp.arange(n_assign, dtype=jnp.int32) // TOP_K
    slot_tok = jnp.zeros((n_blocks * MOE_BLOCK,), jnp.int32).at[dest].set(flat_tok)
    block_start = jnp.arange(n_blocks, dtype=jnp.int32) * MOE_BLOCK
    block_e = jnp.minimum(jnp.sum(block_start[:, None] >= pad_end[None, :], axis=1), N_EXPERTS - 1).astype(jnp.int32)
    n_active = (pad_end[-1] // MOE_BLOCK).astype(jnp.int32).reshape(1)
    return slot_tok, dest, block_e, n_active


def kernel(x_prompt, x_sample, mem_prompt, cache_mla_ckv, cache_mla_krope, state_ret, cache_mem_k, cache_mem_v, norm_mix, norm_xattn, norm_ffn, norm_final, mla_w_dq, mla_g_q, mla_w_uq, mla_w_dkv, mla_g_kv, mla_w_uk, mla_w_uv, mla_w_o, ret_w_q, ret_w_k, ret_w_v, ret_w_g, ret_w_o, xa_w_q, xa_w_k, xa_w_v, xa_w_o, ffn_w1, ffn_w3, ffn_w2, moe_w_router, moe_w1, moe_w3, moe_w2):
    bp, lp, d = x_prompt.shape
    bs, ls, _ = x_sample.shape
    past = cache_mla_ckv.shape[2]
    n_mem = mem_prompt.shape[1]
    tp, ts = bp * lp, bs * ls
    t = tp + ts
    heads = mla_w_uk.shape[2]
    q_rank, kv_rank = mla_w_dq.shape[2], mla_w_uk.shape[1]
    dk = d // RET_HEADS
    dv = 2 * dk
    xa_dim = XA_HEADS * XA_HEAD_DIM

    pos = jnp.concatenate([jnp.tile(jnp.arange(lp, dtype=jnp.int32), bp),
                           jnp.tile(past + jnp.arange(ls, dtype=jnp.int32), bs)])
    h0 = jnp.concatenate([x_prompt.reshape(tp, d), x_sample.reshape(ts, d)], axis=0)
    tm_x = _tile(lp, (128, 64, 32))

    def cross_attention(h, layer, k_p, v_p, xn_dtype):
        wq, wo = xa_w_q[layer].astype(BF16), xa_w_o[layer].astype(BF16)
        k_all = jnp.concatenate([k_p, cache_mem_k[layer].reshape(bs * n_mem, xa_dim)], axis=0).astype(BF16)
        v_all = jnp.concatenate([v_p, cache_mem_v[layer].reshape(bs * n_mem, xa_dim)], axis=0).astype(BF16)
        return xattn_block(h, norm_xattn[layer], wq, k_all, v_all, wo, norm_ffn[layer], n_long=bp, len_long=lp,
                           len_short=ls, n_mem=n_mem, tm=tm_x, xn_dtype=xn_dtype)

    def memory_kv(layer):
        w = jnp.concatenate([xa_w_k[layer], xa_w_v[layer]], axis=1).astype(BF16)
        kv = matmul(mem_prompt.reshape(bp * n_mem, d).astype(BF16), w, out_dtype=F32, name="mem_kv")
        return kv[:, :xa_dim], kv[:, xa_dim:]

    xn = rmsnorm(h0, norm_mix[0], BF16)
    cos_m, sin_m = _rope_half_tables(pos)
    w_dkv = mla_w_dkv[0]
    w_kr = w_dkv[:, kv_rank:]
    w_pre = jnp.concatenate([mla_w_dq[0], w_dkv[:, :kv_rank], _pad_lanes(w_kr), _pad_lanes(_rot_half_cols(w_kr))],
                            axis=1).astype(BF16)
    cq, ckv, ckv_b, kr, kr_b = mla_pre(xn, w_pre, mla_g_q[0], mla_g_kv[0], cos_m, sin_m)

    w_uq = mla_w_uq[0].reshape(q_rank, heads, MLA_NOPE + MLA_ROPE)
    wn = w_uq[:, :, :MLA_NOPE].reshape(q_rank, heads * LANES).astype(BF16)
    w_rope = w_uq[:, :, MLA_NOPE:]
    wr = jnp.concatenate([w_rope, _rot_half_cols(w_rope)], axis=-1).reshape(q_rank, heads * LANES).astype(BF16)
    cs_m = jnp.concatenate([cos_m[:, :MLA_ROPE], sin_m[:, :MLA_ROPE]], axis=-1)
    q = mla_q(cq, wn, wr, cs_m, scale=float((MLA_NOPE + MLA_ROPE) ** -0.5 * 1.4426950408889634))

    w_ukv = jnp.concatenate([mla_w_uk[0].reshape(kv_rank, heads * MLA_NOPE),
                             mla_w_uv[0].reshape(kv_rank, heads * MLA_V)], axis=1).astype(BF16)
    kv = matmul(ckv_b, w_ukv, out_dtype=BF16, name="mla_kv")
    t_att = _tile(lp, (1024, 512, 256, 128, 64))
    o_p = flash_attention(q, kv, kr_b, batch=bp, heads=heads, lq=lp, lk=lp, lk_valid=lp, q_off=0,
                          tq=t_att, tk=t_att)

    lk_s = past + ls
    lk_pad = -(-lk_s // LANES) * LANES
    ckv_all = jnp.concatenate([cache_mla_ckv[0].astype(BF16), ckv_b[tp:].reshape(bs, ls, kv_rank),
                               jnp.zeros((bs, lk_pad - lk_s, kv_rank), BF16)], axis=1)
    kr_all = jnp.concatenate([_pad_lanes(cache_mla_krope[0]).astype(BF16), kr_b[tp:].reshape(bs, ls, LANES),
                              jnp.zeros((bs, lk_pad - lk_s, LANES), BF16)], axis=1)
    kv_s = matmul(ckv_all.reshape(bs * lk_pad, kv_rank), w_ukv, out_dtype=BF16, name="mla_kv_sample")
    o_s = flash_attention(q[tp:], kv_s, kr_all.reshape(bs * lk_pad, LANES), batch=bs, heads=heads, lq=ls,
                          lk=lk_pad, lk_valid=lk_s, q_off=past, tq=ls, tk=lk_pad)
    o = jnp.concatenate([o_p, o_s], axis=0)
    h1 = matmul(o, mla_w_o[0].astype(BF16), out_dtype=F32, res=h0, name="mla_out")

    mk0, mv0 = memory_kv(0)
    h2, xn_f = cross_attention(h1, 0, mk0, mv0, BF16)

    one_block = jnp.zeros((t // _tile(t, (1280, 640, 256, 128, 64, 32)),), jnp.int32)
    all_active = jnp.full((1,), one_block.shape[0], jnp.int32)
    d_ff = ffn_w1.shape[2]
    hf = expert_gateup(xn_f, ffn_w1.astype(BF16), ffn_w3.astype(BF16), one_block, all_active,
                       tm=t // one_block.shape[0], tf=_tile(d_ff, (512, 256, 128)))
    h3 = matmul(hf, ffn_w2[0].astype(BF16), out_dtype=F32, res=h2, name="ffn_down")

    xn = rmsnorm(h3, norm_mix[1], BF16)
    cos_r, sin_r = _rope_pair_tables(pos, dk // 2)
    w_qk = jnp.concatenate([_even_odd_cols(ret_w_q[0], RET_HEADS, dk),
                            _even_odd_cols(ret_w_k[0] * (dk ** -0.5), RET_HEADS, dk)], axis=1).astype(BF16)
    qk = ret_qk(xn, w_qk, cos_r, sin_r, dk=dk)
    rv = matmul(xn, ret_w_v[0].astype(BF16), out_dtype=BF16, name="ret_v")
    rg = matmul(xn, ret_w_g[0].astype(BF16), out_dtype=BF16, name="ret_g")
    log_g = jnp.log1p(-jnp.exp2(-5.0 - jnp.arange(RET_HEADS, dtype=F32)))
    ch_p = _tile(lp, (256, 128, 64))
    og_p, st_p = retention(qk, rv, rg, jnp.zeros((bp, RET_HEADS, dk, dv), F32), log_g, batch=bp, heads=RET_HEADS,
                           length=lp, chunk=ch_p, row0=0)
    og_s, st_s = retention(qk, rv, rg, _even_odd_rows(state_ret[0]), log_g, batch=bs, heads=RET_HEADS,
                           length=ls, chunk=ls, row0=tp)
    og = jnp.concatenate([og_p, og_s], axis=0)
    h4 = matmul(og, ret_w_o[0].astype(BF16), out_dtype=F32, res=h3, name="ret_out")

    mk1, mv1 = memory_kv(1)
    h5, xn_f = cross_attention(h4, 1, mk1, mv1, F32)

    idx_l, gate_l = moe_router(xn_f, moe_w_router[0])
    slot_tok, dest, block_e, n_active = _route(idx_l[:, :TOP_K], t)
    xs = moe_gather(xn_f, slot_tok, tm=MOE_BLOCK)
    hs = expert_gateup(xs, moe_w1[0], moe_w3[0], block_e, n_active, tm=MOE_BLOCK, tf=_tile(d_ff, (512, 256, 128)))
    yb = expert_down(hs, moe_w2[0].astype(BF16), block_e, n_active, tm=MOE_BLOCK, tn=_tile(d, (512, 256, 128)))
    y_p, y_s = moe_combine(yb, dest, h5, gate_l, norm_final, tm=_tile(ts, (128, 64, 32, 16, 8)), t_first=tp)

    def split(a, width):
        return a[:tp].reshape(1, bp, lp, width), a[tp:].reshape(1, bs, ls, width)

    ckv_p, ckv_s = split(ckv, kv_rank)
    kr_p, kr_s = split(kr, MLA_ROPE)
    mem_shape = (bp, n_mem, XA_HEADS, XA_HEAD_DIM)
    return (y_p.reshape(bp, lp, d), y_s.reshape(bs, ls, d),
            ckv_p, kr_p, _even_odd_rows(st_p, inverse=True)[None],
            jnp.stack([mk0.reshape(mem_shape), mk1.reshape(mem_shape)]),
            jnp.stack([mv0.reshape(mem_shape), mv1.reshape(mem_shape)]),
            ckv_s, kr_s, _even_odd_rows(st_s, inverse=True)[None])
```

```python
import functools

import jax
import jax.numpy as jnp
from jax import lax
from jax.experimental import pallas as pl
from jax.experimental.pallas import tpu as pltpu

F32 = jnp.float32
BF16 = jnp.bfloat16

RMS_EPS = 1e-6
CHUNK = 64
MLA_NOPE = 128
MLA_ROPE = 64
MLA_V = 128
MLA_ROPE_BASE = 10000.0
RET_HEADS = 16
RET_ROPE_BASE = 10000.0
XA_HEADS = 4
XA_HEAD_DIM = 128
N_EXPERTS = 8
TOP_K = 2
MOE_BLOCK = 256
MOE_SUPER = 1024
DMA_UNROLL = 8
LANES = 128
NEG_BIG = -1e30
VMEM_LIMIT_BYTES = 56 * 1024 * 1024


def _params(*sem):
    return pltpu.CompilerParams(dimension_semantics=sem, vmem_limit_bytes=VMEM_LIMIT_BYTES)


def _tile(n, candidates):
    for c in candidates:
        if c <= n and n % c == 0:
            return c
    return n


def _rms(x, g=None):
    y = x * lax.rsqrt(jnp.mean(x * x, axis=-1, keepdims=True) + RMS_EPS)
    return y if g is None else y * g


def _rmsnorm_kernel(x_ref, g_ref, o_ref):
    o_ref[...] = _rms(x_ref[...], g_ref[...]).astype(o_ref.dtype)


def rmsnorm(x, g, out_dtype):
    t, d = x.shape
    tm = _tile(t, (256, 128, 64, 32, 16, 8))
    return pl.pallas_call(
        _rmsnorm_kernel,
        out_shape=jax.ShapeDtypeStruct((t, d), out_dtype),
        grid=(t // tm,),
        in_specs=[pl.BlockSpec((tm, d), lambda i: (i, 0)), pl.BlockSpec((1, d), lambda i: (0, 0))],
        out_specs=pl.BlockSpec((tm, d), lambda i: (i, 0)),
        compiler_params=_params("parallel"),
        name="rmsnorm",
    )(x, g.reshape(1, d))


def _mm_kernel(*refs, nk, has_res):
    a_ref, w_ref = refs[0], refs[1]
    res_ref = refs[2] if has_res else None
    o_ref = refs[2 + has_res]
    part = jnp.dot(a_ref[...], w_ref[...], preferred_element_type=F32)
    if nk == 1:
        if has_res:
            part = part + res_ref[...]
        o_ref[...] = part.astype(o_ref.dtype)
        return
    acc_ref = refs[3 + has_res]
    k = pl.program_id(2)

    @pl.when(k == 0)
    def _():
        acc_ref[...] = part

    @pl.when(k > 0)
    def _():
        acc_ref[...] += part

    @pl.when(k == nk - 1)
    def _():
        out = acc_ref[...]
        if has_res:
            out = out + res_ref[...]
        o_ref[...] = out.astype(o_ref.dtype)


def matmul(a, w, *, out_dtype, res=None, tm=None, tn=None, tk=None, name="matmul"):
    m, kdim = a.shape
    n = w.shape[1]
    tk = tk or (kdim if kdim <= 4096 else _tile(kdim, (2048, 1024, 512)))
    row_tiles = (1280, 1024) if tk <= 2048 else ()
    tm = tm or _tile(m, row_tiles + (640, 512, 256, 128, 64, 32, 16, 8))
    tn = tn or _tile(n, (1024, 512, 256, 128))
    nk = kdim // tk
    has_res = res is not None
    in_specs = [pl.BlockSpec((tm, tk), lambda j, i, k: (i, k)),
                pl.BlockSpec((tk, tn), lambda j, i, k: (k, j))]
    args = [a, w]
    if has_res:
        in_specs.append(pl.BlockSpec((tm, tn), lambda j, i, k: (i, j)))
        args.append(res)
    scratch = [pltpu.VMEM((tm, tn), F32)] if nk > 1 else []
    return pl.pallas_call(
        functools.partial(_mm_kernel, nk=nk, has_res=has_res),
        out_shape=jax.ShapeDtypeStruct((m, n), out_dtype),
        grid=(n // tn, m // tm, nk),
        in_specs=in_specs,
        out_specs=pl.BlockSpec((tm, tn), lambda j, i, k: (i, j)),
        scratch_shapes=scratch,
        compiler_params=_params("parallel", "parallel", "arbitrary"),
        name=name,
    )(*args)


def _mla_pre_kernel(x_ref, w_ref, gq_ref, gkv_ref, cos_ref, sin_ref,
                    cq_ref, ckv_ref, ckvb_ref, kr_ref, krb_ref, *, q_rank, kv_rank):
    r = jnp.dot(x_ref[...], w_ref[...], preferred_element_type=F32)
    cq_ref[...] = _rms(r[:, :q_rank], gq_ref[...]).astype(cq_ref.dtype)
    ckv = _rms(r[:, q_rank:q_rank + kv_rank], gkv_ref[...])
    ckv_ref[...] = ckv
    ckvb_ref[...] = ckv.astype(ckvb_ref.dtype)
    o = q_rank + kv_rank
    kr = r[:, o:o + LANES] * cos_ref[...] + r[:, o + LANES:o + 2 * LANES] * sin_ref[...]
    kr_ref[...] = kr[:, :MLA_ROPE]
    krb_ref[...] = kr.astype(krb_ref.dtype)


def mla_pre(xn, w_pre, g_q, g_kv, cos_t, sin_t):
    t, d = xn.shape
    q_rank, kv_rank = g_q.shape[0], g_kv.shape[0]
    n = w_pre.shape[1]
    tm = _tile(t, (256, 128, 64, 32, 16, 8))
    row = lambda i: (i, 0)
    fixed = lambda i: (0, 0)
    return pl.pallas_call(
        functools.partial(_mla_pre_kernel, q_rank=q_rank, kv_rank=kv_rank),
        out_shape=(jax.ShapeDtypeStruct((t, q_rank), BF16),
                   jax.ShapeDtypeStruct((t, kv_rank), F32),
                   jax.ShapeDtypeStruct((t, kv_rank), BF16),
                   jax.ShapeDtypeStruct((t, MLA_ROPE), F32),
                   jax.ShapeDtypeStruct((t, LANES), BF16)),
        grid=(t // tm,),
        in_specs=[pl.BlockSpec((tm, d), row), pl.BlockSpec((d, n), fixed),
                  pl.BlockSpec((1, q_rank), fixed), pl.BlockSpec((1, kv_rank), fixed),
                  pl.BlockSpec((tm, LANES), row), pl.BlockSpec((tm, LANES), row)],
        out_specs=(pl.BlockSpec((tm, q_rank), row), pl.BlockSpec((tm, kv_rank), row),
                   pl.BlockSpec((tm, kv_rank), row), pl.BlockSpec((tm, MLA_ROPE), row),
                   pl.BlockSpec((tm, LANES), row)),
        compiler_params=_params("parallel"),
        name="mla_pre",
    )(xn, w_pre, g_q.reshape(1, -1), g_kv.reshape(1, -1), cos_t, sin_t)


def _mla_q_kernel(c_ref, wn_ref, wr_ref, cs_ref, q_ref, *, heads_per_step, scale):
    c = c_ref[...]
    qn = jnp.dot(c, wn_ref[...], preferred_element_type=F32) * scale
    qr = jnp.dot(c, wr_ref[...], preferred_element_type=F32)
    cs = cs_ref[...] * scale
    for h in range(heads_per_step):
        lo, hi = h * LANES, (h + 1) * LANES
        q_ref[:, 2 * lo:2 * lo + LANES] = qn[:, lo:hi].astype(q_ref.dtype)
        terms = qr[:, lo:hi] * cs
        q_ref[:, 2 * lo + LANES:2 * hi] = (terms + pltpu.roll(terms, LANES // 2, 1)).astype(q_ref.dtype)


def mla_q(cq, wn, wr, cs_t, *, scale):
    t, q_rank = cq.shape
    n = wn.shape[1]
    tm = _tile(t, (1280, 1024, 640, 512, 256, 128, 64, 32, 16, 8))
    tn = _tile(n, (512, 256, 128))
    w_spec = pl.BlockSpec((q_rank, tn), lambda j, i: (0, j))
    return pl.pallas_call(
        functools.partial(_mla_q_kernel, heads_per_step=tn // LANES, scale=scale),
        out_shape=jax.ShapeDtypeStruct((t, 2 * n), BF16),
        grid=(n // tn, t // tm),
        in_specs=[pl.BlockSpec((tm, q_rank), lambda j, i: (i, 0)), w_spec, w_spec,
                  pl.BlockSpec((tm, LANES), lambda j, i: (i, 0))],
        out_specs=pl.BlockSpec((tm, 2 * tn), lambda j, i: (i, j)),
        compiler_params=_params("parallel", "parallel"),
        name="mla_q",
    )(cq, wn, wr, cs_t)


def _flash_last_k(qi, *, tq, tk, nk, q_off):
    last_pos = ((q_off + (qi + 1) * tq - 1) // CHUNK) * CHUNK + CHUNK - 1
    return min(nk - 1, last_pos // tk)


def _flash_kernel(qi_ref, ki_ref, q_ref, kn_ref, kr_ref, v_ref, o_ref, m_scr, l_scr, acc_scr,
                  *, tq, tk, rb, hp, nk, q_off, lk_valid, aligned):
    step = pl.program_id(2)
    qi, ki = qi_ref[step], ki_ref[step]
    first_q = q_off + qi * tq
    last_seen_by_first_row = (first_q // CHUNK) * CHUNK + CHUNK - 1
    tile_last_key = ki * tk + tk - 1
    needs_mask = jnp.logical_or(tile_last_key > last_seen_by_first_row, tile_last_key >= lk_valid)
    last_ki = jnp.minimum(nk - 1, (((first_q + tq - 1) // CHUNK) * CHUNK + CHUNK - 1) // tk)
    shift = CHUNK.bit_length() - 1

    @pl.when(ki == 0)
    def _():
        m_scr[...] = jnp.full(m_scr.shape, NEG_BIG, F32)
        l_scr[...] = jnp.zeros(l_scr.shape, F32)
        acc_scr[...] = jnp.zeros(acc_scr.shape, F32)

    def attend(masked):
        block = rb if (masked and aligned) else tq
        for h in range(hp):
            hl = slice(h * LANES, (h + 1) * LANES)
            for r in range(tq // block):
                rows = slice(r * block, (r + 1) * block)
                nc = min(tk, (r + 1) * block) if (masked and aligned) else tk
                k = jnp.concatenate([kn_ref[:nc, hl], kr_ref[:nc, :]], axis=-1)
                s = lax.dot_general(q_ref[rows, 2 * h * LANES:2 * (h + 1) * LANES], k, (((1,), (1,)), ((), ())),
                                    preferred_element_type=F32)
                if masked:
                    qpos = first_q + r * block + lax.broadcasted_iota(jnp.int32, (block, nc), 0)
                    kpos = ki * tk + lax.broadcasted_iota(jnp.int32, (block, nc), 1)
                    s = jnp.where(lax.shift_right_arithmetic(kpos, shift) <= lax.shift_right_arithmetic(qpos, shift),
                                  s, NEG_BIG)
                    if lk_valid < nk * tk:
                        s = jnp.where(kpos < lk_valid, s, NEG_BIG)
                m_prev = m_scr[h, rows, :]
                m_new = jnp.maximum(m_prev, jnp.max(s, axis=-1, keepdims=True))
                alpha = jnp.exp2(m_prev - m_new)
                p = jnp.exp2(s - jnp.tile(m_new, (1, nc // LANES)))
                l_scr[h, rows, :] = alpha * l_scr[h, rows, :] + jnp.sum(p, axis=-1, keepdims=True)
                acc_scr[rows, hl] = alpha * acc_scr[rows, hl] + jnp.dot(p.astype(BF16), v_ref[:nc, hl],
                                                                       preferred_element_type=F32)
                m_scr[h, rows, :] = m_new

    pl.when(needs_mask)(functools.partial(attend, True))
    pl.when(jnp.logical_not(needs_mask))(functools.partial(attend, False))

    @pl.when(ki == last_ki)
    def _():
        for h in range(hp):
            hl = slice(h * LANES, (h + 1) * LANES)
            o_ref[:, hl] = (acc_scr[:, hl] / l_scr[h]).astype(o_ref.dtype)


def flash_attention(q, kv, kr, *, batch, heads, lq, lk, lk_valid, q_off, tq, tk):
    assert CHUNK & (CHUNK - 1) == 0 and MLA_V == LANES and MLA_NOPE == LANES
    nq, nk = lq // tq, lk // tk
    pairs = [(qi, ki) for qi in range(nq)
             for ki in range(_flash_last_k(qi, tq=tq, tk=tk, nk=nk, q_off=q_off) + 1)]
    qi_of = jnp.asarray([p[0] for p in pairs], jnp.int32)
    ki_of = jnp.asarray([p[1] for p in pairs], jnp.int32)
    hp = 2 if heads % 2 == 0 else 1
    hg = heads // hp
    rb = _tile(tq, (256, 128, 64, 32, 16))
    aligned = q_off % tq == 0 and tq == tk and lk_valid == lk and rb % max(CHUNK, LANES) == 0
    grid_spec = pltpu.PrefetchScalarGridSpec(
        num_scalar_prefetch=2,
        grid=(batch, hg, len(pairs)),
        in_specs=[pl.BlockSpec((tq, 2 * hp * LANES), lambda b, h, s, qi, ki: (b * nq + qi[s], h)),
                  pl.BlockSpec((tk, hp * LANES), lambda b, h, s, qi, ki: (b * nk + ki[s], h)),
                  pl.BlockSpec((tk, LANES), lambda b, h, s, qi, ki: (b * nk + ki[s], 0)),
                  pl.BlockSpec((tk, hp * LANES), lambda b, h, s, qi, ki: (b * nk + ki[s], hg + h))],
        out_specs=pl.BlockSpec((tq, hp * LANES), lambda b, h, s, qi, ki: (b * nq + qi[s], h)),
        scratch_shapes=[pltpu.VMEM((hp, tq, LANES), F32), pltpu.VMEM((hp, tq, LANES), F32),
                        pltpu.VMEM((tq, hp * LANES), F32)],
    )
    return pl.pallas_call(
        functools.partial(_flash_kernel, tq=tq, tk=tk, rb=rb, hp=hp, nk=nk, q_off=q_off, lk_valid=lk_valid,
                          aligned=aligned),
        out_shape=jax.ShapeDtypeStruct((batch * lq, heads * MLA_V), BF16),
        grid_spec=grid_spec,
        compiler_params=_params("parallel", "parallel", "arbitrary"),
        name="mla_flash",
    )(qi_of, ki_of, q, kv, kr, kv)


def _xattn_attend(q, k, v):
    scale = float(XA_HEAD_DIM ** -0.5)
    outs = []
    for hd in range(XA_HEADS):
        sl = slice(hd * XA_HEAD_DIM, (hd + 1) * XA_HEAD_DIM)
        s = lax.dot_general(q[:, sl], k[:, sl], (((1,), (1,)), ((), ())), preferred_element_type=F32) * scale
        p = jnp.exp(s - jnp.max(s, axis=-1, keepdims=True))
        p = p / jnp.sum(p, axis=-1, keepdims=True)
        outs.append(jnp.dot(p.astype(BF16), v[:, sl], preferred_element_type=F32).astype(BF16))
    return jnp.concatenate(outs, axis=-1)


def _xattn_kernel(*refs, sub, n_long_tiles):
    h_ref, gx_ref, wq_ref, wo_ref, gf_ref = refs[:5]
    k_refs, v_refs = refs[5:5 + sub], refs[5 + sub:5 + 2 * sub]
    h_out, xn_out, o_scr = refs[5 + 2 * sub:]
    i = pl.program_id(0)
    x = h_ref[...]
    xn = _rms(x, gx_ref[...]).astype(BF16)
    q = jnp.dot(xn, wq_ref[...], preferred_element_type=F32).astype(BF16)
    rows = q.shape[0] // sub

    @pl.when(i < n_long_tiles)
    def _():
        o_scr[...] = _xattn_attend(q, k_refs[0][...], v_refs[0][...])

    @pl.when(i >= n_long_tiles)
    def _():
        for j in range(sub):
            o_scr[j * rows:(j + 1) * rows, :] = _xattn_attend(q[j * rows:(j + 1) * rows], k_refs[j][...], v_refs[j][...])

    h_new = x + jnp.dot(o_scr[...], wo_ref[...], preferred_element_type=F32)
    h_out[...] = h_new
    xn_out[...] = _rms(h_new, gf_ref[...]).astype(xn_out.dtype)


def xattn_block(h, g_x, w_q, k, v, w_o, g_f, *, n_long, len_long, len_short, n_mem, tm, xn_dtype):
    t, d = h.shape
    hd = XA_HEADS * XA_HEAD_DIM
    sub = tm // len_short
    tiles_per_long = len_long // tm
    n_long_tiles = n_long * tiles_per_long
    row = lambda i: (i, 0)
    fixed = lambda i: (0, 0)

    def mem(j):
        def index(i):
            short = n_long + (i - n_long_tiles) * sub + j
            return (jnp.where(i < n_long_tiles, i // tiles_per_long, short), 0)
        return pl.BlockSpec((n_mem, hd), index)

    in_specs = ([pl.BlockSpec((tm, d), row), pl.BlockSpec((1, d), fixed), pl.BlockSpec((d, hd), fixed),
                 pl.BlockSpec((hd, d), fixed), pl.BlockSpec((1, d), fixed)]
                + [mem(j) for j in range(sub)] * 2)
    return pl.pallas_call(
        functools.partial(_xattn_kernel, sub=sub, n_long_tiles=n_long_tiles),
        out_shape=(jax.ShapeDtypeStruct((t, d), F32), jax.ShapeDtypeStruct((t, d), xn_dtype)),
        grid=(t // tm,),
        in_specs=in_specs,
        out_specs=(pl.BlockSpec((tm, d), row), pl.BlockSpec((tm, d), row)),
        scratch_shapes=[pltpu.VMEM((tm, hd), BF16)],
        compiler_params=_params("parallel"),
        name="xattn",
    )(h, g_x.reshape(1, d), w_q, w_o, g_f.reshape(1, d), *([k] * sub), *([v] * sub))


def _swiglu(x, w1, w3):
    a = jnp.dot(x, w1, preferred_element_type=F32)
    b = jnp.dot(x, w3, preferred_element_type=F32)
    return (a * jax.nn.sigmoid(a) * b).astype(BF16)


def _dense_gateup_kernel(x_ref, w1_ref, w3_ref, o_ref):
    o_ref[...] = _swiglu(x_ref[...], w1_ref[...], w3_ref[...])


def dense_gateup(x, w1, w3, *, tm, tf):
    t, d = x.shape
    f = w1.shape[1]
    w_spec = pl.BlockSpec((d, tf), lambda j, i: (0, j))
    return pl.pallas_call(
        _dense_gateup_kernel,
        out_shape=jax.ShapeDtypeStruct((t, f), BF16),
        grid=(f // tf, t // tm),
        in_specs=[pl.BlockSpec((tm, d), lambda j, i: (i, 0)), w_spec, w_spec],
        out_specs=pl.BlockSpec((tm, tf), lambda j, i: (i, j)),
        compiler_params=_params("parallel", "parallel"),
        name="dense_gateup",
    )(x, w1, w3)


def _moe_gateup_kernel(se_ref, nact_ref, src_ref, x_ref, w1_ref, w3_ref, o_ref, *, sub, n_sub):
    nact = nact_ref[pl.program_id(0)]

    @pl.when(nact == n_sub)
    def _():
        o_ref[...] = _swiglu(x_ref[...], w1_ref[...].astype(BF16), w3_ref[...].astype(BF16))

    @pl.when(nact < n_sub)
    def _():
        for b in range(n_sub):
            rows = slice(b * sub, (b + 1) * sub)

            @pl.when(b < nact)
            def _():
                o_ref[rows, :] = _swiglu(x_ref[rows, :], w1_ref[...].astype(BF16), w3_ref[...].astype(BF16))

            @pl.when(b >= nact)
            def _():
                o_ref[rows, :] = jnp.zeros((sub, o_ref.shape[1]), o_ref.dtype)


def moe_gateup(xs, w1, w3, sup_e, sup_nact, sup_src, *, tm, tf):
    s, d = xs.shape
    f = w1.shape[2]
    w_spec = pl.BlockSpec((None, d, tf), lambda i, j, se, na, src: (se[i], 0, jnp.where(na[i] > 0, j, 0)))
    grid_spec = pltpu.PrefetchScalarGridSpec(
        num_scalar_prefetch=3,
        grid=(s // tm, f // tf),
        in_specs=[pl.BlockSpec((tm, d), lambda i, j, se, na, src: (src[i], 0), pipeline_mode=pl.Buffered(1)),
                  w_spec, w_spec],
        out_specs=pl.BlockSpec((tm, tf), lambda i, j, se, na, src: (i, j)),
    )
    return pl.pallas_call(
        functools.partial(_moe_gateup_kernel, sub=MOE_BLOCK, n_sub=tm // MOE_BLOCK),
        out_shape=jax.ShapeDtypeStruct((s, f), BF16),
        grid_spec=grid_spec,
        compiler_params=_params("parallel", "arbitrary"),
        name="moe_gateup",
    )(sup_e, sup_nact, sup_src, xs, w1, w3)


def _moe_down_kernel(be_ref, act_ref, src_ref, h_ref, w_ref, o_ref):
    i = pl.program_id(1)

    @pl.when(act_ref[i] == 1)
    def _():
        o_ref[...] = jnp.dot(h_ref[...], w_ref[...], preferred_element_type=F32)

    @pl.when(act_ref[i] == 0)
    def _():
        o_ref[...] = jnp.zeros(o_ref.shape, o_ref.dtype)


def moe_down(hs, w2, blk_e, blk_act, blk_src, *, tm, tn):
    s, f = hs.shape
    d = w2.shape[2]
    grid_spec = pltpu.PrefetchScalarGridSpec(
        num_scalar_prefetch=3,
        grid=(d // tn, s // tm),
        in_specs=[pl.BlockSpec((tm, f), lambda j, i, be, act, src: (src[i], 0)),
                  pl.BlockSpec((None, f, tn), lambda j, i, be, act, src: (be[i], 0, j))],
        out_specs=pl.BlockSpec((tm, tn), lambda j, i, be, act, src: (i, j)),
    )
    return pl.pallas_call(
        _moe_down_kernel,
        out_shape=jax.ShapeDtypeStruct((s, d), F32),
        grid_spec=grid_spec,
        compiler_params=_params("parallel", "arbitrary"),
        name="moe_down",
    )(blk_e, blk_act, blk_src, hs, w2)


def _router_kernel(x_ref, wh_ref, wl_ref, idx_ref, gate_ref):
    x = x_ref[...]
    xh = x.astype(BF16)
    xl = (x - xh.astype(F32)).astype(BF16)
    wh, wl = wh_ref[...], wl_ref[...]
    logits = (jnp.dot(xh, wh, preferred_element_type=F32) + jnp.dot(xl, wh, preferred_element_type=F32)
              + jnp.dot(xh, wl, preferred_element_type=F32))
    lane = lax.broadcasted_iota(jnp.int32, logits.shape, 1)
    logits = jnp.where(lane < N_EXPERTS, logits, -jnp.inf)
    m1 = jnp.max(logits, axis=-1, keepdims=True)
    i1 = jnp.min(jnp.where(logits == m1, lane, LANES), axis=-1, keepdims=True)
    rest = jnp.where(lane == i1, -jnp.inf, logits)
    m2 = jnp.max(rest, axis=-1, keepdims=True)
    i2 = jnp.min(jnp.where(rest == m2, lane, LANES), axis=-1, keepdims=True)
    e = jnp.exp(m2 - m1)
    g1 = 1.0 / (1.0 + e)
    g2 = e / (1.0 + e)
    idx_ref[...] = jnp.where(lane == 0, i1, jnp.where(lane == 1, i2, 0))
    gate_ref[...] = jnp.where(lane == 0, g1, jnp.where(lane == 1, g2, 0.0))


def moe_router(xn, w_router):
    t, d = xn.shape
    w = jnp.zeros((d, LANES), F32).at[:, :N_EXPERTS].set(w_router)
    wh = w.astype(BF16)
    wl = (w - wh.astype(F32)).astype(BF16)
    tm = _tile(t, (256, 128, 64, 32, 16, 8))
    row = lambda i: (i, 0)
    fixed = lambda i: (0, 0)
    return pl.pallas_call(
        _router_kernel,
        out_shape=(jax.ShapeDtypeStruct((t, LANES), jnp.int32), jax.ShapeDtypeStruct((t, LANES), F32)),
        grid=(t // tm,),
        in_specs=[pl.BlockSpec((tm, d), row), pl.BlockSpec((d, LANES), fixed), pl.BlockSpec((d, LANES), fixed)],
        out_specs=(pl.BlockSpec((tm, LANES), row), pl.BlockSpec((tm, LANES), row)),
        compiler_params=_params("parallel"),
        name="moe_router",
    )(xn, wh, wl)


def _row_dma_loop(n_rows, make_copy, wait):
    def body(g, c):
        for u in range(DMA_UNROLL):
            copy = make_copy(g * DMA_UNROLL + u)
            if wait:
                copy.wait()
            else:
                copy.start(priority=u % 2)
        return c

    lax.fori_loop(0, n_rows // DMA_UNROLL, body, 0)


def _gather_kernel(tok_ref, act_ref, x_hbm, o_ref, buf, sem, *, tm, n_blocks):
    i = pl.program_id(0)

    def rows_of(blk, slot):
        def make_copy(r):
            return pltpu.make_async_copy(x_hbm.at[pl.ds(tok_ref[blk * tm + r], 1)],
                                         buf.at[slot, pl.ds(r, 1)], sem.at[slot])
        return make_copy

    @pl.when(jnp.logical_and(i == 0, act_ref[0] == 1))
    def _():
        _row_dma_loop(tm, rows_of(0, 0), wait=False)

    nxt = jnp.minimum(i + 1, n_blocks - 1)

    @pl.when(jnp.logical_and(i + 1 < n_blocks, act_ref[nxt] == 1))
    def _():
        _row_dma_loop(tm, rows_of(nxt, nxt % 2), wait=False)

    @pl.when(act_ref[i] == 1)
    def _():
        _row_dma_loop(tm, rows_of(i, i % 2), wait=True)
        o_ref[...] = buf[i % 2].astype(o_ref.dtype)

    @pl.when(act_ref[i] == 0)
    def _():
        o_ref[...] = jnp.zeros(o_ref.shape, o_ref.dtype)


def moe_gather(xn, slot_tok, blk_act, *, tm):
    s = slot_tok.shape[0]
    d = xn.shape[1]
    assert tm % DMA_UNROLL == 0
    grid_spec = pltpu.PrefetchScalarGridSpec(
        num_scalar_prefetch=2,
        grid=(s // tm,),
        in_specs=[pl.BlockSpec(memory_space=pl.ANY)],
        out_specs=pl.BlockSpec((tm, d), lambda i, tok, act: (i, 0)),
        scratch_shapes=[pltpu.VMEM((2, tm, d), F32), pltpu.SemaphoreType.DMA((2,))],
    )
    return pl.pallas_call(
        functools.partial(_gather_kernel, tm=tm, n_blocks=s // tm),
        out_shape=jax.ShapeDtypeStruct((s, d), BF16),
        grid_spec=grid_spec,
        compiler_params=_params("arbitrary"),
        name="moe_gather",
    )(slot_tok, blk_act, xn)


def _combine_kernel(pos_ref, y_hbm, h_ref, gate_ref, gf_ref, o_first_ref, o_rest_ref, buf, sem,
                    *, tm, n_first, n_blocks):
    i = pl.program_id(0)

    def rows_of(blk, slot):
        def make_copy(r):
            return pltpu.make_async_copy(y_hbm.at[pl.ds(pos_ref[blk * tm * TOP_K + r], 1)],
                                         buf.at[slot, r % TOP_K, pl.ds(r // TOP_K, 1)], sem.at[slot])
        return make_copy

    @pl.when(i == 0)
    def _():
        _row_dma_loop(tm * TOP_K, rows_of(0, 0), wait=False)

    @pl.when(i + 1 < n_blocks)
    def _():
        _row_dma_loop(tm * TOP_K, rows_of(i + 1, (i + 1) % 2), wait=False)

    _row_dma_loop(tm * TOP_K, rows_of(i, i % 2), wait=True)
    gate = gate_ref[...]
    slot = i % 2
    y = buf[slot, 0] * gate[:, 0:1] + buf[slot, 1] * gate[:, 1:2]
    out = _rms(h_ref[...] + y, gf_ref[...])

    @pl.when(i < n_first)
    def _():
        o_first_ref[...] = out

    @pl.when(i >= n_first)
    def _():
        o_rest_ref[...] = out


def moe_combine(yb, pos, h, gate, g_final, *, tm, t_first):
    t, d = h.shape
    n_first = t_first // tm
    grid_spec = pltpu.PrefetchScalarGridSpec(
        num_scalar_prefetch=1,
        grid=(t // tm,),
        in_specs=[pl.BlockSpec(memory_space=pl.ANY),
                  pl.BlockSpec((tm, d), lambda i, pos: (i, 0)),
                  pl.BlockSpec((tm, LANES), lambda i, pos: (i, 0)),
                  pl.BlockSpec((1, d), lambda i, pos: (0, 0))],
        out_specs=(pl.BlockSpec((tm, d), lambda i, pos: (jnp.minimum(i, n_first - 1), 0)),
                   pl.BlockSpec((tm, d), lambda i, pos: (jnp.maximum(i - n_first, 0), 0))),
        scratch_shapes=[pltpu.VMEM((2, TOP_K, tm, d), F32), pltpu.SemaphoreType.DMA((2,))],
    )
    assert (tm * TOP_K) % DMA_UNROLL == 0 and DMA_UNROLL % TOP_K == 0
    return pl.pallas_call(
        functools.partial(_combine_kernel, tm=tm, n_first=n_first, n_blocks=t // tm),
        out_shape=(jax.ShapeDtypeStruct((t_first, d), F32), jax.ShapeDtypeStruct((t - t_first, d), F32)),
        grid_spec=grid_spec,
        compiler_params=_params("arbitrary"),
        name="moe_combine",
    )(pos, yb, h, gate, g_final.reshape(1, d))


def _ret_qk_kernel(x_ref, w_ref, cos_ref, sin_ref, o_ref, *, heads_per_step, dk):
    r = jnp.dot(x_ref[...], w_ref[...], preferred_element_type=F32)
    cos, sin = cos_ref[...], sin_ref[...]
    half = dk // 2
    for h in range(heads_per_step):
        e = r[:, h * dk:h * dk + half]
        o = r[:, h * dk + half:(h + 1) * dk]
        o_ref[:, h * dk:h * dk + half] = (e * cos - o * sin).astype(o_ref.dtype)
        o_ref[:, h * dk + half:(h + 1) * dk] = (o * cos + e * sin).astype(o_ref.dtype)


def ret_qk(xn, w_qk, cos_t, sin_t, *, dk):
    t, d = xn.shape
    n = w_qk.shape[1]
    tm = _tile(t, (640, 512, 256, 128, 64, 32, 16, 8))
    tn = _tile(n, (1024, 512, 256))
    return pl.pallas_call(
        functools.partial(_ret_qk_kernel, heads_per_step=tn // dk, dk=dk),
        out_shape=jax.ShapeDtypeStruct((t, n), BF16),
        grid=(n // tn, t // tm),
        in_specs=[pl.BlockSpec((tm, d), lambda j, i: (i, 0)), pl.BlockSpec((d, tn), lambda j, i: (0, j)),
                  pl.BlockSpec((tm, dk // 2), lambda j, i: (i, 0)),
                  pl.BlockSpec((tm, dk // 2), lambda j, i: (i, 0))],
        out_specs=pl.BlockSpec((tm, tn), lambda j, i: (i, j)),
        compiler_params=_params("parallel", "parallel"),
        name="ret_qk",
    )(xn, w_qk, cos_t, sin_t)


def _ret_kernel(q_ref, k_ref, v_ref, g_ref, s0_ref, dmat_ref, qdec_ref, kdec_ref, o_ref, s_out_ref, s_scr,
                *, nc, hps, dk, dv):
    c = pl.program_id(2)

    @pl.when(c == 0)
    def _():
        s_scr[...] = s0_ref[...]

    for h in range(hps):
        ks, vs = slice(h * dk, (h + 1) * dk), slice(h * dv, (h + 1) * dv)
        q, k, v = q_ref[:, ks], k_ref[:, ks], v_ref[:, vs]
        state = s_scr[h]
        qdec = qdec_ref[h][:, 0:1]
        kdec = kdec_ref[h][:, 0:1]
        scores = lax.dot_general(q, k, (((1,), (1,)), ((), ())), preferred_element_type=F32) * dmat_ref[h]
        intra = jnp.dot(scores.astype(BF16), v, preferred_element_type=F32)
        inter = jnp.dot(q, state.astype(BF16), preferred_element_type=F32) * qdec
        o = intra + inter
        kd = (k.astype(F32) * kdec).astype(BF16)
        chunk_decay = qdec[q.shape[0] - 1:, :]
        new_state = chunk_decay * state + lax.dot_general(kd, v, (((0,), (0,)), ((), ())),
                                                          preferred_element_type=F32)
        s_scr[h] = new_state
        g = g_ref[:, vs].astype(F32)
        o_ref[:, vs] = (g * jax.nn.sigmoid(g) * _rms(o)).astype(o_ref.dtype)

    @pl.when(c == nc - 1)
    def _():
        s_out_ref[...] = s_scr[...]


def retention(qk, v, g, s0, log_g, *, batch, heads, length, chunk, row0):
    dk, dv = s0.shape[2], s0.shape[3]
    nc = length // chunk
    blk0 = row0 // chunk
    idx = jnp.arange(chunk, dtype=F32)
    diff = idx[:, None] - idx[None, :]
    dmat = jnp.exp(jnp.where(diff[None] >= 0, diff[None] * log_g[:, None, None], -jnp.inf))
    qdec = jnp.broadcast_to(jnp.exp((idx[None, :] + 1.0) * log_g[:, None])[:, :, None], (heads, chunk, LANES))
    kdec = jnp.broadcast_to(jnp.exp((chunk - 1.0 - idx)[None, :] * log_g[:, None])[:, :, None], (heads, chunk, LANES))
    row = lambda b, h, c: blk0 + b * nc + c
    hps = _tile(heads, (4, 2, 1))
    hg = heads // hps
    return pl.pallas_call(
        functools.partial(_ret_kernel, nc=nc, hps=hps, dk=dk, dv=dv),
        out_shape=(jax.ShapeDtypeStruct((batch * length, heads * dv), BF16),
                   jax.ShapeDtypeStruct((batch, heads, dk, dv), F32)),
        grid=(batch, hg, nc),
        in_specs=[pl.BlockSpec((chunk, hps * dk), lambda b, h, c: (row(b, h, c), h)),
                  pl.BlockSpec((chunk, hps * dk), lambda b, h, c: (row(b, h, c), hg + h)),
                  pl.BlockSpec((chunk, hps * dv), lambda b, h, c: (row(b, h, c), h)),
                  pl.BlockSpec((chunk, hps * dv), lambda b, h, c: (row(b, h, c), h)),
                  pl.BlockSpec((None, hps, dk, dv), lambda b, h, c: (b, h, 0, 0)),
                  pl.BlockSpec((hps, chunk, chunk), lambda b, h, c: (h, 0, 0)),
                  pl.BlockSpec((hps, chunk, LANES), lambda b, h, c: (h, 0, 0)),
                  pl.BlockSpec((hps, chunk, LANES), lambda b, h, c: (h, 0, 0))],
        out_specs=(pl.BlockSpec((chunk, hps * dv), lambda b, h, c: (b * nc + c, h)),
                   pl.BlockSpec((None, hps, dk, dv), lambda b, h, c: (b, h, 0, 0))),
        scratch_shapes=[pltpu.VMEM((hps, dk, dv), F32)],
        compiler_params=_params("parallel", "parallel", "arbitrary"),
        name="retention",
    )(qk, qk, v, g, s0, dmat, qdec, kdec)


def _rope_half_tables(pos):
    half = MLA_ROPE // 2
    inv = 1.0 / (MLA_ROPE_BASE ** (jnp.arange(half, dtype=F32) / half))
    ang = pos.astype(F32)[:, None] * inv[None, :]
    c, s = jnp.cos(ang), jnp.sin(ang)
    z = jnp.zeros((pos.shape[0], LANES - MLA_ROPE), F32)
    return jnp.concatenate([c, c, z], axis=-1), jnp.concatenate([s, s, z], axis=-1)


def _rope_pair_tables(pos, half):
    inv = 1.0 / (RET_ROPE_BASE ** jnp.linspace(0.0, 1.0, half, dtype=F32))
    ang = pos.astype(F32)[:, None] * inv[None, :]
    return jnp.cos(ang), jnp.sin(ang)


def _rot_half_cols(w):
    half = w.shape[-1] // 2
    return jnp.concatenate([-w[..., half:], w[..., :half]], axis=-1)


def _pad_lanes(w):
    return jnp.pad(w, [(0, 0)] * (w.ndim - 1) + [(0, LANES - w.shape[-1])])


def _even_odd_cols(w, heads, dk):
    d = w.shape[0]
    return w.reshape(d, heads, dk // 2, 2).transpose(0, 1, 3, 2).reshape(d, heads * dk)


def _even_odd_rows(s, inverse=False):
    b, h, dk, dv = s.shape
    if inverse:
        return s.reshape(b, h, 2, dk // 2, dv).transpose(0, 1, 3, 2, 4).reshape(b, h, dk, dv)
    return s.reshape(b, h, dk // 2, 2, dv).transpose(0, 1, 3, 2, 4).reshape(b, h, dk, dv)


def _last_active(act):
    return lax.cummax(jnp.where(act > 0, jnp.arange(act.shape[0], dtype=jnp.int32), 0), axis=0)


def _route(idx, n_tok):
    n_assign = n_tok * TOP_K
    flat_e = idx.reshape(-1)
    onehot = (flat_e[:, None] == jnp.arange(N_EXPERTS, dtype=jnp.int32)[None, :]).astype(jnp.int32)
    csum = jnp.cumsum(onehot, axis=0)
    rank = jnp.sum((csum - onehot) * onehot, axis=1)
    counts = csum[-1]
    padded = (counts + MOE_SUPER - 1) // MOE_SUPER * MOE_SUPER
    pad_end = jnp.cumsum(padded)
    pad_start = pad_end - padded
    dest = (jnp.sum(pad_start[None, :] * onehot, axis=1) + rank).astype(jnp.int32)
    n_super = (n_assign + N_EXPERTS * (MOE_SUPER - 1)) // MOE_SUPER
    per_super = MOE_SUPER // MOE_BLOCK
    flat_tok = jnp.arange(n_assign, dtype=jnp.int32) // TOP_K
    slot_tok = jnp.zeros((n_super * MOE_SUPER,), jnp.int32).at[dest].set(flat_tok)
    sup_start = jnp.arange(n_super, dtype=jnp.int32) * MOE_SUPER
    sup_e = jnp.minimum(jnp.sum(sup_start[:, None] >= pad_end[None, :], axis=1), N_EXPERTS - 1).astype(jnp.int32)
    used = jnp.clip(counts[sup_e] - (sup_start - pad_start[sup_e]), 0, MOE_SUPER)
    sup_nact = ((used + MOE_BLOCK - 1) // MOE_BLOCK).astype(jnp.int32)
    blk_e = jnp.repeat(sup_e, per_super)
    blk_act = (jnp.arange(per_super, dtype=jnp.int32)[None, :] < sup_nact[:, None]).astype(jnp.int32).reshape(-1)
    return slot_tok, dest, (sup_e, sup_nact, _last_active(sup_nact)), (blk_e, blk_act, _last_active(blk_act))


def kernel(x_prompt, x_sample, mem_prompt, cache_mla_ckv, cache_mla_krope, state_ret, cache_mem_k, cache_mem_v, norm_mix, norm_xattn, norm_ffn, norm_final, mla_w_dq, mla_g_q, mla_w_uq, mla_w_dkv, mla_g_kv, mla_w_uk, mla_w_uv, mla_w_o, ret_w_q, ret_w_k, ret_w_v, ret_w_g, ret_w_o, xa_w_q, xa_w_k, xa_w_v, xa_w_o, ffn_w1, ffn_w3, ffn_w2, moe_w_router, moe_w1, moe_w3, moe_w2):
    bp, lp, d = x_prompt.shape
    bs, ls, _ = x_sample.shape
    past = cache_mla_ckv.shape[2]
    n_mem = mem_prompt.shape[1]
    tp, ts = bp * lp, bs * ls
    t = tp + ts
    heads = mla_w_uk.shape[2]
    q_rank, kv_rank = mla_w_dq.shape[2], mla_w_uk.shape[1]
    dk = d // RET_HEADS
    dv = 2 * dk
    xa_dim = XA_HEADS * XA_HEAD_DIM

    pos = jnp.concatenate([jnp.tile(jnp.arange(lp, dtype=jnp.int32), bp),
                           jnp.tile(past + jnp.arange(ls, dtype=jnp.int32), bs)])
    h0 = jnp.concatenate([x_prompt.reshape(tp, d), x_sample.reshape(ts, d)], axis=0)
    tm_x = _tile(lp, (128, 64, 32))

    def cross_attention(h, layer, k_p, v_p, xn_dtype):
        wq, wo = xa_w_q[layer].astype(BF16), xa_w_o[layer].astype(BF16)
        k_all = jnp.concatenate([k_p, cache_mem_k[layer].reshape(bs * n_mem, xa_dim)], axis=0).astype(BF16)
        v_all = jnp.concatenate([v_p, cache_mem_v[layer].reshape(bs * n_mem, xa_dim)], axis=0).astype(BF16)
        return xattn_block(h, norm_xattn[layer], wq, k_all, v_all, wo, norm_ffn[layer], n_long=bp, len_long=lp,
                           len_short=ls, n_mem=n_mem, tm=tm_x, xn_dtype=xn_dtype)

    def memory_kv(layer):
        w = jnp.concatenate([xa_w_k[layer], xa_w_v[layer]], axis=1).astype(BF16)
        kv = matmul(mem_prompt.reshape(bp * n_mem, d).astype(BF16), w, out_dtype=F32, name="mem_kv")
        return kv[:, :xa_dim], kv[:, xa_dim:]

    xn = rmsnorm(h0, norm_mix[0], BF16)
    cos_m, sin_m = _rope_half_tables(pos)
    w_dkv = mla_w_dkv[0]
    w_kr = w_dkv[:, kv_rank:]
    w_pre = jnp.concatenate([mla_w_dq[0], w_dkv[:, :kv_rank], _pad_lanes(w_kr), _pad_lanes(_rot_half_cols(w_kr))],
                            axis=1).astype(BF16)
    cq, ckv, ckv_b, kr, kr_b = mla_pre(xn, w_pre, mla_g_q[0], mla_g_kv[0], cos_m, sin_m)

    w_uq = mla_w_uq[0].reshape(q_rank, heads, MLA_NOPE + MLA_ROPE)
    wn = w_uq[:, :, :MLA_NOPE].reshape(q_rank, heads * LANES).astype(BF16)
    w_rope = w_uq[:, :, MLA_NOPE:]
    wr = jnp.concatenate([w_rope, _rot_half_cols(w_rope)], axis=-1).reshape(q_rank, heads * LANES).astype(BF16)
    cs_m = jnp.concatenate([cos_m[:, :MLA_ROPE], sin_m[:, :MLA_ROPE]], axis=-1)
    q = mla_q(cq, wn, wr, cs_m, scale=float((MLA_NOPE + MLA_ROPE) ** -0.5 * 1.4426950408889634))

    w_ukv = jnp.concatenate([mla_w_uk[0].reshape(kv_rank, heads * MLA_NOPE),
                             mla_w_uv[0].reshape(kv_rank, heads * MLA_V)], axis=1).astype(BF16)
    kv = matmul(ckv_b, w_ukv, out_dtype=BF16, name="mla_kv")
    t_att = _tile(lp, (1024, 512, 256, 128, 64))
    o_p = flash_attention(q, kv, kr_b, batch=bp, heads=heads, lq=lp, lk=lp, lk_valid=lp, q_off=0,
                          tq=t_att, tk=t_att)

    lk_s = past + ls
    lk_pad = -(-lk_s // LANES) * LANES
    ckv_all = jnp.concatenate([cache_mla_ckv[0].astype(BF16), ckv_b[tp:].reshape(bs, ls, kv_rank),
                               jnp.zeros((bs, lk_pad - lk_s, kv_rank), BF16)], axis=1)
    kr_all = jnp.concatenate([_pad_lanes(cache_mla_krope[0]).astype(BF16), kr_b[tp:].reshape(bs, ls, LANES),
                              jnp.zeros((bs, lk_pad - lk_s, LANES), BF16)], axis=1)
    kv_s = matmul(ckv_all.reshape(bs * lk_pad, kv_rank), w_ukv, out_dtype=BF16, name="mla_kv_sample")
    o_s = flash_attention(q[tp:], kv_s, kr_all.reshape(bs * lk_pad, LANES), batch=bs, heads=heads, lq=ls,
                          lk=lk_pad, lk_valid=lk_s, q_off=past, tq=ls, tk=lk_pad)
    o = jnp.concatenate([o_p, o_s], axis=0)
    h1 = matmul(o, mla_w_o[0].astype(BF16), out_dtype=F32, res=h0, name="mla_out")

    mk0, mv0 = memory_kv(0)
    h2, xn_f = cross_attention(h1, 0, mk0, mv0, BF16)

    d_ff = ffn_w1.shape[2]
    hf = dense_gateup(xn_f, ffn_w1[0].astype(BF16), ffn_w3[0].astype(BF16),
                      tm=_tile(t, (1280, 640, 256, 128, 64, 32)), tf=_tile(d_ff, (512, 256, 128)))
    h3 = matmul(hf, ffn_w2[0].astype(BF16), out_dtype=F32, res=h2, name="ffn_down")

    xn = rmsnorm(h3, norm_mix[1], BF16)
    cos_r, sin_r = _rope_pair_tables(pos, dk // 2)
    w_qk = jnp.concatenate([_even_odd_cols(ret_w_q[0], RET_HEADS, dk),
                            _even_odd_cols(ret_w_k[0] * (dk ** -0.5), RET_HEADS, dk)], axis=1).astype(BF16)
    qk = ret_qk(xn, w_qk, cos_r, sin_r, dk=dk)
    rv = matmul(xn, ret_w_v[0].astype(BF16), out_dtype=BF16, name="ret_v")
    rg = matmul(xn, ret_w_g[0].astype(BF16), out_dtype=BF16, name="ret_g")
    log_g = jnp.log1p(-jnp.exp2(-5.0 - jnp.arange(RET_HEADS, dtype=F32)))
    ch_p = _tile(lp, (256, 128, 64))
    og_p, st_p = retention(qk, rv, rg, jnp.zeros((bp, RET_HEADS, dk, dv), F32), log_g, batch=bp, heads=RET_HEADS,
                           length=lp, chunk=ch_p, row0=0)
    og_s, st_s = retention(qk, rv, rg, _even_odd_rows(state_ret[0]), log_g, batch=bs, heads=RET_HEADS,
                           length=ls, chunk=ls, row0=tp)
    og = jnp.concatenate([og_p, og_s], axis=0)
    h4 = matmul(og, ret_w_o[0].astype(BF16), out_dtype=F32, res=h3, name="ret_out")

    mk1, mv1 = memory_kv(1)
    h5, xn_f = cross_attention(h4, 1, mk1, mv1, F32)

    idx_l, gate_l = moe_router(xn_f, moe_w_router[0])
    slot_tok, dest, sup, blk = _route(idx_l[:, :TOP_K], t)
    xs = moe_gather(xn_f, slot_tok, blk[1], tm=MOE_BLOCK)
    hs = moe_gateup(xs, moe_w1[0], moe_w3[0], *sup, tm=MOE_SUPER, tf=_tile(d_ff, (512, 256, 128)))
    yb = moe_down(hs, moe_w2[0].astype(BF16), *blk, tm=MOE_BLOCK, tn=_tile(d, (512, 256, 128)))
    y_p, y_s = moe_combine(yb, dest, h5, gate_l, norm_final, tm=_tile(ts, (128, 64, 32, 16, 8)), t_first=tp)

    def split(a, width):
        return a[:tp].reshape(1, bp, lp, width), a[tp:].reshape(1, bs, ls, width)

    ckv_p, ckv_s = split(ckv, kv_rank)
    kr_p, kr_s = split(kr, MLA_ROPE)
    mem_shape = (bp, n_mem, XA_HEADS, XA_HEAD_DIM)
    return (y_p.reshape(bp, lp, d), y_s.reshape(bs, ls, d),
            ckv_p, kr_p, _even_odd_rows(st_p, inverse=True)[None],
            jnp.stack([mk0.reshape(mem_shape), mk1.reshape(mem_shape)]),
            jnp.stack([mv0.reshape(mem_shape), mv1.reshape(mem_shape)]),
            ckv_s, kr_s, _even_odd_rows(st_s, inverse=True)[None])
```

```python
import functools

import jax
import jax.numpy as jnp
from jax import lax
from jax.experimental import pallas as pl
from jax.experimental.pallas import tpu as pltpu

F32 = jnp.float32
BF16 = jnp.bfloat16

RMS_EPS = 1e-6
CHUNK = 64
MLA_NOPE = 128
MLA_ROPE = 64
MLA_V = 128
MLA_ROPE_BASE = 10000.0
RET_HEADS = 16
RET_ROPE_BASE = 10000.0
XA_HEADS = 4
XA_HEAD_DIM = 128
N_EXPERTS = 8
TOP_K = 2
MOE_BLOCK = 256
MOE_SUPER = 1024
DMA_UNROLL = 8
LANES = 128
NEG_BIG = -1e30
VMEM_LIMIT_BYTES = 56 * 1024 * 1024


def _params(*sem):
    return pltpu.CompilerParams(dimension_semantics=sem, vmem_limit_bytes=VMEM_LIMIT_BYTES)


def _tile(n, candidates):
    for c in candidates:
        if c <= n and n % c == 0:
            return c
    return n


def _rms(x, g=None):
    y = x * lax.rsqrt(jnp.mean(x * x, axis=-1, keepdims=True) + RMS_EPS)
    return y if g is None else y * g


def _two_part_specs(tm, d, n_first):
    return [pl.BlockSpec((tm, d), lambda i: (jnp.minimum(i, n_first - 1), 0)),
            pl.BlockSpec((tm, d), lambda i: (jnp.maximum(i - n_first, 0), 0))]


def _two_part_tile(first_ref, rest_ref, n_first):
    return jnp.where(pl.program_id(0) < n_first, first_ref[...], rest_ref[...])


def _rmsnorm_kernel(x_ref, g_ref, o_ref):
    o_ref[...] = _rms(x_ref[...], g_ref[...]).astype(o_ref.dtype)


def _rmsnorm2_kernel(xa_ref, xb_ref, g_ref, o_ref, *, n_first):
    o_ref[...] = _rms(_two_part_tile(xa_ref, xb_ref, n_first), g_ref[...]).astype(o_ref.dtype)


def rmsnorm(x, g, out_dtype):
    first, rest = x if isinstance(x, tuple) else (x, None)
    d = first.shape[1]
    t = first.shape[0] + (0 if rest is None else rest.shape[0])
    tm = _tile(t if rest is None else rest.shape[0], (256, 128, 64, 32, 16, 8))
    row = pl.BlockSpec((tm, d), lambda i: (i, 0))
    gain = pl.BlockSpec((1, d), lambda i: (0, 0))
    if rest is None:
        body, in_specs, args = _rmsnorm_kernel, [row, gain], (first,)
    else:
        n_first = first.shape[0] // tm
        body = functools.partial(_rmsnorm2_kernel, n_first=n_first)
        in_specs, args = _two_part_specs(tm, d, n_first) + [gain], (first, rest)
    return pl.pallas_call(
        body,
        out_shape=jax.ShapeDtypeStruct((t, d), out_dtype),
        grid=(t // tm,),
        in_specs=in_specs,
        out_specs=row,
        compiler_params=_params("parallel"),
        name="rmsnorm",
    )(*args, g.reshape(1, d))


def _mm_kernel(*refs, nk, has_res):
    a_ref, w_ref = refs[0], refs[1]
    res_ref = refs[2] if has_res else None
    o_ref = refs[2 + has_res]
    part = jnp.dot(a_ref[...], w_ref[...], preferred_element_type=F32)
    if nk == 1:
        if has_res:
            part = part + res_ref[...]
        o_ref[...] = part.astype(o_ref.dtype)
        return
    acc_ref = refs[3 + has_res]
    k = pl.program_id(2)

    @pl.when(k == 0)
    def _():
        acc_ref[...] = part

    @pl.when(k > 0)
    def _():
        acc_ref[...] += part

    @pl.when(k == nk - 1)
    def _():
        out = acc_ref[...]
        if has_res:
            out = out + res_ref[...]
        o_ref[...] = out.astype(o_ref.dtype)


def matmul(a, w, *, out_dtype, res=None, res_row0=0, tm=None, tn=None, tk=None, name="matmul"):
    m, kdim = a.shape
    n = w.shape[1]
    tk = tk or (kdim if kdim <= 4096 else _tile(kdim, (2048, 1024, 512)))
    row_tiles = (1280, 1024) if tk <= 2048 else ()
    tm = tm or _tile(m, row_tiles + (640, 512, 256, 128, 64, 32, 16, 8))
    tn = tn or _tile(n, (1024, 512, 256, 128))
    nk = kdim // tk
    has_res = res is not None
    in_specs = [pl.BlockSpec((tm, tk), lambda j, i, k: (i, k)),
                pl.BlockSpec((tk, tn), lambda j, i, k: (k, j))]
    args = [a, w]
    if has_res:
        assert res_row0 % tm == 0
        in_specs.append(pl.BlockSpec((tm, tn), lambda j, i, k: (i + res_row0 // tm, j)))
        args.append(res)
    scratch = [pltpu.VMEM((tm, tn), F32)] if nk > 1 else []
    return pl.pallas_call(
        functools.partial(_mm_kernel, nk=nk, has_res=has_res),
        out_shape=jax.ShapeDtypeStruct((m, n), out_dtype),
        grid=(n // tn, m // tm, nk),
        in_specs=in_specs,
        out_specs=pl.BlockSpec((tm, tn), lambda j, i, k: (i, j)),
        scratch_shapes=scratch,
        compiler_params=_params("parallel", "parallel", "arbitrary"),
        name=name,
    )(*args)


def _mla_pre_kernel(x_ref, w_ref, gq_ref, gkv_ref, cos_ref, sin_ref,
                    cq_ref, ckv_ref, ckvb_ref, kr_ref, krb_ref, *, q_rank, kv_rank):
    r = jnp.dot(x_ref[...], w_ref[...], preferred_element_type=F32)
    cq_ref[...] = _rms(r[:, :q_rank], gq_ref[...]).astype(cq_ref.dtype)
    ckv = _rms(r[:, q_rank:q_rank + kv_rank], gkv_ref[...])
    ckv_ref[...] = ckv
    ckvb_ref[...] = ckv.astype(ckvb_ref.dtype)
    o = q_rank + kv_rank
    kr = r[:, o:o + LANES] * cos_ref[...] + r[:, o + LANES:o + 2 * LANES] * sin_ref[...]
    kr_ref[...] = kr[:, :MLA_ROPE]
    krb_ref[...] = kr.astype(krb_ref.dtype)


def mla_pre(xn, w_pre, g_q, g_kv, cos_t, sin_t):
    t, d = xn.shape
    q_rank, kv_rank = g_q.shape[0], g_kv.shape[0]
    n = w_pre.shape[1]
    tm = _tile(t, (256, 128, 64, 32, 16, 8))
    row = lambda i: (i, 0)
    fixed = lambda i: (0, 0)
    return pl.pallas_call(
        functools.partial(_mla_pre_kernel, q_rank=q_rank, kv_rank=kv_rank),
        out_shape=(jax.ShapeDtypeStruct((t, q_rank), BF16),
                   jax.ShapeDtypeStruct((t, kv_rank), F32),
                   jax.ShapeDtypeStruct((t, kv_rank), BF16),
                   jax.ShapeDtypeStruct((t, MLA_ROPE), F32),
                   jax.ShapeDtypeStruct((t, LANES), BF16)),
        grid=(t // tm,),
        in_specs=[pl.BlockSpec((tm, d), row), pl.BlockSpec((d, n), fixed),
                  pl.BlockSpec((1, q_rank), fixed), pl.BlockSpec((1, kv_rank), fixed),
                  pl.BlockSpec((tm, LANES), row), pl.BlockSpec((tm, LANES), row)],
        out_specs=(pl.BlockSpec((tm, q_rank), row), pl.BlockSpec((tm, kv_rank), row),
                   pl.BlockSpec((tm, kv_rank), row), pl.BlockSpec((tm, MLA_ROPE), row),
                   pl.BlockSpec((tm, LANES), row)),
        compiler_params=_params("parallel"),
        name="mla_pre",
    )(xn, w_pre, g_q.reshape(1, -1), g_kv.reshape(1, -1), cos_t, sin_t)


def _mla_q_kernel(c_ref, wn_ref, wr_ref, cs_ref, q_ref, *, heads_per_step, scale):
    c = c_ref[...]
    qn = jnp.dot(c, wn_ref[...], preferred_element_type=F32) * scale
    qr = jnp.dot(c, wr_ref[...], preferred_element_type=F32)
    cs = cs_ref[...] * scale
    for h in range(heads_per_step):
        lo, hi = h * LANES, (h + 1) * LANES
        q_ref[:, 2 * lo:2 * lo + LANES] = qn[:, lo:hi].astype(q_ref.dtype)
        terms = qr[:, lo:hi] * cs
        q_ref[:, 2 * lo + LANES:2 * hi] = (terms + pltpu.roll(terms, LANES // 2, 1)).astype(q_ref.dtype)


def mla_q(cq, wn, wr, cs_t, *, scale):
    t, q_rank = cq.shape
    n = wn.shape[1]
    tm = _tile(t, (1280, 1024, 640, 512, 256, 128, 64, 32, 16, 8))
    tn = _tile(n, (512, 256, 128))
    w_spec = pl.BlockSpec((q_rank, tn), lambda j, i: (0, j))
    return pl.pallas_call(
        functools.partial(_mla_q_kernel, heads_per_step=tn // LANES, scale=scale),
        out_shape=jax.ShapeDtypeStruct((t, 2 * n), BF16),
        grid=(n // tn, t // tm),
        in_specs=[pl.BlockSpec((tm, q_rank), lambda j, i: (i, 0)), w_spec, w_spec,
                  pl.BlockSpec((tm, LANES), lambda j, i: (i, 0))],
        out_specs=pl.BlockSpec((tm, 2 * tn), lambda j, i: (i, j)),
        compiler_params=_params("parallel", "parallel"),
        name="mla_q",
    )(cq, wn, wr, cs_t)


def _flash_last_k(qi, *, tq, tk, nk, q_off):
    last_pos = ((q_off + (qi + 1) * tq - 1) // CHUNK) * CHUNK + CHUNK - 1
    return min(nk - 1, last_pos // tk)


def _flash_kernel(qi_ref, ki_ref, q_ref, kn_ref, kr_ref, v_ref, o_ref, m_scr, l_scr, acc_scr,
                  *, tq, tk, rb, hp, nk, q_off, lk_valid, aligned):
    step = pl.program_id(2)
    qi, ki = qi_ref[step], ki_ref[step]
    first_q = q_off + qi * tq
    last_seen_by_first_row = (first_q // CHUNK) * CHUNK + CHUNK - 1
    tile_last_key = ki * tk + tk - 1
    needs_mask = jnp.logical_or(tile_last_key > last_seen_by_first_row, tile_last_key >= lk_valid)
    last_ki = jnp.minimum(nk - 1, (((first_q + tq - 1) // CHUNK) * CHUNK + CHUNK - 1) // tk)
    shift = CHUNK.bit_length() - 1

    @pl.when(ki == 0)
    def _():
        m_scr[...] = jnp.full(m_scr.shape, NEG_BIG, F32)
        l_scr[...] = jnp.zeros(l_scr.shape, F32)
        acc_scr[...] = jnp.zeros(acc_scr.shape, F32)

    def attend(masked):
        block = rb if (masked and aligned) else tq
        for h in range(hp):
            hl = slice(h * LANES, (h + 1) * LANES)
            for r in range(tq // block):
                rows = slice(r * block, (r + 1) * block)
                nc = min(tk, (r + 1) * block) if (masked and aligned) else tk
                k = jnp.concatenate([kn_ref[:nc, hl], kr_ref[:nc, :]], axis=-1)
                s = lax.dot_general(q_ref[rows, 2 * h * LANES:2 * (h + 1) * LANES], k, (((1,), (1,)), ((), ())),
                                    preferred_element_type=F32)
                if masked:
                    qpos = first_q + r * block + lax.broadcasted_iota(jnp.int32, (block, nc), 0)
                    kpos = ki * tk + lax.broadcasted_iota(jnp.int32, (block, nc), 1)
                    s = jnp.where(lax.shift_right_arithmetic(kpos, shift) <= lax.shift_right_arithmetic(qpos, shift),
                                  s, NEG_BIG)
                    if lk_valid < nk * tk:
                        s = jnp.where(kpos < lk_valid, s, NEG_BIG)
                m_prev = m_scr[h, rows, :]
                m_new = jnp.maximum(m_prev, jnp.max(s, axis=-1, keepdims=True))
                alpha = jnp.exp2(m_prev - m_new)
                p = jnp.exp2(s - jnp.tile(m_new, (1, nc // LANES)))
                l_scr[h, rows, :] = alpha * l_scr[h, rows, :] + jnp.sum(p, axis=-1, keepdims=True)
                acc_scr[rows, hl] = alpha * acc_scr[rows, hl] + jnp.dot(p.astype(BF16), v_ref[:nc, hl],
                                                                       preferred_element_type=F32)
                m_scr[h, rows, :] = m_new

    pl.when(needs_mask)(functools.partial(attend, True))
    pl.when(jnp.logical_not(needs_mask))(functools.partial(attend, False))

    @pl.when(ki == last_ki)
    def _():
        for h in range(hp):
            hl = slice(h * LANES, (h + 1) * LANES)
            o_ref[:, hl] = (acc_scr[:, hl] / l_scr[h]).astype(o_ref.dtype)


def flash_attention(q, kv, kr, *, batch, heads, lq, lk, lk_valid, q_off, tq, tk):
    assert CHUNK & (CHUNK - 1) == 0 and MLA_V == LANES and MLA_NOPE == LANES
    nq, nk = lq // tq, lk // tk
    pairs = [(qi, ki) for qi in range(nq)
             for ki in range(_flash_last_k(qi, tq=tq, tk=tk, nk=nk, q_off=q_off) + 1)]
    qi_of = jnp.asarray([p[0] for p in pairs], jnp.int32)
    ki_of = jnp.asarray([p[1] for p in pairs], jnp.int32)
    hp = _tile(heads, (4, 2, 1))
    hg = heads // hp
    rb = _tile(tq, (256, 128, 64, 32, 16))
    aligned = q_off % tq == 0 and tq == tk and lk_valid == lk and rb % max(CHUNK, LANES) == 0
    grid_spec = pltpu.PrefetchScalarGridSpec(
        num_scalar_prefetch=2,
        grid=(batch, hg, len(pairs)),
        in_specs=[pl.BlockSpec((tq, 2 * hp * LANES), lambda b, h, s, qi, ki: (b * nq + qi[s], h)),
                  pl.BlockSpec((tk, hp * LANES), lambda b, h, s, qi, ki: (b * nk + ki[s], h)),
                  pl.BlockSpec((tk, LANES), lambda b, h, s, qi, ki: (b * nk + ki[s], 0)),
                  pl.BlockSpec((tk, hp * LANES), lambda b, h, s, qi, ki: (b * nk + ki[s], hg + h))],
        out_specs=pl.BlockSpec((tq, hp * LANES), lambda b, h, s, qi, ki: (b * nq + qi[s], h)),
        scratch_shapes=[pltpu.VMEM((hp, tq, LANES), F32), pltpu.VMEM((hp, tq, LANES), F32),
                        pltpu.VMEM((tq, hp * LANES), F32)],
    )
    return pl.pallas_call(
        functools.partial(_flash_kernel, tq=tq, tk=tk, rb=rb, hp=hp, nk=nk, q_off=q_off, lk_valid=lk_valid,
                          aligned=aligned),
        out_shape=jax.ShapeDtypeStruct((batch * lq, heads * MLA_V), BF16),
        grid_spec=grid_spec,
        compiler_params=_params("parallel", "parallel", "arbitrary"),
        name="mla_flash",
    )(qi_of, ki_of, q, kv, kr, kv)


def _xattn_attend(q, k, v):
    scale = float(XA_HEAD_DIM ** -0.5)
    outs = []
    for hd in range(XA_HEADS):
        sl = slice(hd * XA_HEAD_DIM, (hd + 1) * XA_HEAD_DIM)
        s = lax.dot_general(q[:, sl], k[:, sl], (((1,), (1,)), ((), ())), preferred_element_type=F32) * scale
        p = jnp.exp(s - jnp.max(s, axis=-1, keepdims=True))
        p = p / jnp.sum(p, axis=-1, keepdims=True)
        outs.append(jnp.dot(p.astype(BF16), v[:, sl], preferred_element_type=F32).astype(BF16))
    return jnp.concatenate(outs, axis=-1)


def _xattn_kernel(*refs, sub, n_long_tiles):
    ha_ref, hb_ref, gx_ref, wq_ref, wo_ref, gf_ref = refs[:6]
    refs = refs[1:]
    k_refs, v_refs = refs[5:5 + sub], refs[5 + sub:5 + 2 * sub]
    h_out, xn_out, o_scr = refs[5 + 2 * sub:]
    i = pl.program_id(0)
    x = _two_part_tile(ha_ref, hb_ref, n_long_tiles)
    xn = _rms(x, gx_ref[...]).astype(BF16)
    q = jnp.dot(xn, wq_ref[...], preferred_element_type=F32).astype(BF16)
    rows = q.shape[0] // sub

    @pl.when(i < n_long_tiles)
    def _():
        o_scr[...] = _xattn_attend(q, k_refs[0][...], v_refs[0][...])

    @pl.when(i >= n_long_tiles)
    def _():
        for j in range(sub):
            o_scr[j * rows:(j + 1) * rows, :] = _xattn_attend(q[j * rows:(j + 1) * rows], k_refs[j][...], v_refs[j][...])

    h_new = x + jnp.dot(o_scr[...], wo_ref[...], preferred_element_type=F32)
    h_out[...] = h_new
    xn_out[...] = _rms(h_new, gf_ref[...]).astype(xn_out.dtype)


def xattn_block(h_long, h_short, g_x, w_q, k, v, w_o, g_f, *, n_long, len_long, len_short, n_mem, tm, xn_dtype):
    d = h_long.shape[1]
    t = h_long.shape[0] + h_short.shape[0]
    hd = XA_HEADS * XA_HEAD_DIM
    sub = tm // len_short
    tiles_per_long = len_long // tm
    n_long_tiles = n_long * tiles_per_long
    row = lambda i: (i, 0)
    fixed = lambda i: (0, 0)

    def mem(j):
        def index(i):
            short = n_long + (i - n_long_tiles) * sub + j
            return (jnp.where(i < n_long_tiles, i // tiles_per_long, short), 0)
        return pl.BlockSpec((n_mem, hd), index)

    in_specs = (_two_part_specs(tm, d, n_long_tiles)
                + [pl.BlockSpec((1, d), fixed), pl.BlockSpec((d, hd), fixed),
                   pl.BlockSpec((hd, d), fixed), pl.BlockSpec((1, d), fixed)]
                + [mem(j) for j in range(sub)] * 2)
    return pl.pallas_call(
        functools.partial(_xattn_kernel, sub=sub, n_long_tiles=n_long_tiles),
        out_shape=(jax.ShapeDtypeStruct((t, d), F32), jax.ShapeDtypeStruct((t, d), xn_dtype)),
        grid=(t // tm,),
        in_specs=in_specs,
        out_specs=(pl.BlockSpec((tm, d), row), pl.BlockSpec((tm, d), row)),
        scratch_shapes=[pltpu.VMEM((tm, hd), BF16)],
        compiler_params=_params("parallel"),
        name="xattn",
    )(h_long, h_short, g_x.reshape(1, d), w_q, w_o, g_f.reshape(1, d), *([k] * sub), *([v] * sub))


def _swiglu(x, w1, w3):
    a = jnp.dot(x, w1, preferred_element_type=F32)
    b = jnp.dot(x, w3, preferred_element_type=F32)
    return (a * jax.nn.sigmoid(a) * b).astype(BF16)


def _dense_gateup_kernel(x_ref, w1_ref, w3_ref, o_ref):
    o_ref[...] = _swiglu(x_ref[...], w1_ref[...], w3_ref[...])


def dense_gateup(x, w1, w3, *, tm, tf):
    t, d = x.shape
    f = w1.shape[1]
    w_spec = pl.BlockSpec((d, tf), lambda j, i: (0, j))
    return pl.pallas_call(
        _dense_gateup_kernel,
        out_shape=jax.ShapeDtypeStruct((t, f), BF16),
        grid=(f // tf, t // tm),
        in_specs=[pl.BlockSpec((tm, d), lambda j, i: (i, 0)), w_spec, w_spec],
        out_specs=pl.BlockSpec((tm, tf), lambda j, i: (i, j)),
        compiler_params=_params("parallel", "parallel"),
        name="dense_gateup",
    )(x, w1, w3)


def _moe_gateup_kernel(se_ref, nact_ref, src_ref, x_ref, w1_ref, w3_ref, o_ref, *, sub, n_sub):
    nact = nact_ref[pl.program_id(0)]

    @pl.when(nact == n_sub)
    def _():
        o_ref[...] = _swiglu(x_ref[...], w1_ref[...].astype(BF16), w3_ref[...].astype(BF16))

    @pl.when(nact < n_sub)
    def _():
        for b in range(n_sub):
            rows = slice(b * sub, (b + 1) * sub)

            @pl.when(b < nact)
            def _():
                o_ref[rows, :] = _swiglu(x_ref[rows, :], w1_ref[...].astype(BF16), w3_ref[...].astype(BF16))

            @pl.when(b >= nact)
            def _():
                o_ref[rows, :] = jnp.zeros((sub, o_ref.shape[1]), o_ref.dtype)


def moe_gateup(xs, w1, w3, sup_e, sup_nact, sup_src, *, tm, tf):
    s, d = xs.shape
    f = w1.shape[2]
    w_spec = pl.BlockSpec((None, d, tf), lambda i, j, se, na, src: (se[i], 0, jnp.where(na[i] > 0, j, 0)))
    grid_spec = pltpu.PrefetchScalarGridSpec(
        num_scalar_prefetch=3,
        grid=(s // tm, f // tf),
        in_specs=[pl.BlockSpec((tm, d), lambda i, j, se, na, src: (src[i], 0), pipeline_mode=pl.Buffered(1)),
                  w_spec, w_spec],
        out_specs=pl.BlockSpec((tm, tf), lambda i, j, se, na, src: (i, j)),
    )
    return pl.pallas_call(
        functools.partial(_moe_gateup_kernel, sub=MOE_BLOCK, n_sub=tm // MOE_BLOCK),
        out_shape=jax.ShapeDtypeStruct((s, f), BF16),
        grid_spec=grid_spec,
        compiler_params=_params("parallel", "arbitrary"),
        name="moe_gateup",
    )(sup_e, sup_nact, sup_src, xs, w1, w3)


def _moe_down_kernel(be_ref, act_ref, src_ref, h_ref, w_ref, o_ref):
    i = pl.program_id(1)

    @pl.when(act_ref[i] == 1)
    def _():
        o_ref[...] = jnp.dot(h_ref[...], w_ref[...], preferred_element_type=F32)

    @pl.when(act_ref[i] == 0)
    def _():
        o_ref[...] = jnp.zeros(o_ref.shape, o_ref.dtype)


def moe_down(hs, w2, blk_e, blk_act, blk_src, *, tm, tn):
    s, f = hs.shape
    d = w2.shape[2]
    grid_spec = pltpu.PrefetchScalarGridSpec(
        num_scalar_prefetch=3,
        grid=(d // tn, s // tm),
        in_specs=[pl.BlockSpec((tm, f), lambda j, i, be, act, src: (src[i], 0)),
                  pl.BlockSpec((None, f, tn), lambda j, i, be, act, src: (be[i], 0, j))],
        out_specs=pl.BlockSpec((tm, tn), lambda j, i, be, act, src: (i, j)),
    )
    return pl.pallas_call(
        _moe_down_kernel,
        out_shape=jax.ShapeDtypeStruct((s, d), F32),
        grid_spec=grid_spec,
        compiler_params=_params("parallel", "arbitrary"),
        name="moe_down",
    )(blk_e, blk_act, blk_src, hs, w2)


def _router_kernel(x_ref, wh_ref, wl_ref, idx_ref, gate_ref):
    x = x_ref[...]
    xh = x.astype(BF16)
    xl = (x - xh.astype(F32)).astype(BF16)
    wh, wl = wh_ref[...], wl_ref[...]
    logits = (jnp.dot(xh, wh, preferred_element_type=F32) + jnp.dot(xl, wh, preferred_element_type=F32)
              + jnp.dot(xh, wl, preferred_element_type=F32))
    lane = lax.broadcasted_iota(jnp.int32, logits.shape, 1)
    logits = jnp.where(lane < N_EXPERTS, logits, -jnp.inf)
    m1 = jnp.max(logits, axis=-1, keepdims=True)
    i1 = jnp.min(jnp.where(logits == m1, lane, LANES), axis=-1, keepdims=True)
    rest = jnp.where(lane == i1, -jnp.inf, logits)
    m2 = jnp.max(rest, axis=-1, keepdims=True)
    i2 = jnp.min(jnp.where(rest == m2, lane, LANES), axis=-1, keepdims=True)
    e = jnp.exp(m2 - m1)
    g1 = 1.0 / (1.0 + e)
    g2 = e / (1.0 + e)
    idx_ref[...] = jnp.where(lane == 0, i1, jnp.where(lane == 1, i2, 0))
    gate_ref[...] = jnp.where(lane == 0, g1, jnp.where(lane == 1, g2, 0.0))


def moe_router(xn, w_router):
    t, d = xn.shape
    w = jnp.zeros((d, LANES), F32).at[:, :N_EXPERTS].set(w_router)
    wh = w.astype(BF16)
    wl = (w - wh.astype(F32)).astype(BF16)
    tm = _tile(t, (256, 128, 64, 32, 16, 8))
    row = lambda i: (i, 0)
    fixed = lambda i: (0, 0)
    return pl.pallas_call(
        _router_kernel,
        out_shape=(jax.ShapeDtypeStruct((t, LANES), jnp.int32), jax.ShapeDtypeStruct((t, LANES), F32)),
        grid=(t // tm,),
        in_specs=[pl.BlockSpec((tm, d), row), pl.BlockSpec((d, LANES), fixed), pl.BlockSpec((d, LANES), fixed)],
        out_specs=(pl.BlockSpec((tm, LANES), row), pl.BlockSpec((tm, LANES), row)),
        compiler_params=_params("parallel"),
        name="moe_router",
    )(xn, wh, wl)


def _row_dma_loop(n_rows, make_copy, wait):
    def body(g, c):
        for u in range(DMA_UNROLL):
            copy = make_copy(g * DMA_UNROLL + u)
            if wait:
                copy.wait()
            else:
                copy.start(priority=u % 2)
        return c

    lax.fori_loop(0, n_rows // DMA_UNROLL, body, 0)


def _gather_kernel(tok_ref, act_ref, x_hbm, o_ref, buf, sem, *, tm, n_blocks):
    i = pl.program_id(0)

    def rows_of(blk, slot):
        def make_copy(r):
            return pltpu.make_async_copy(x_hbm.at[pl.ds(tok_ref[blk * tm + r], 1)],
                                         buf.at[slot, pl.ds(r, 1)], sem.at[slot])
        return make_copy

    @pl.when(jnp.logical_and(i == 0, act_ref[0] == 1))
    def _():
        _row_dma_loop(tm, rows_of(0, 0), wait=False)

    nxt = jnp.minimum(i + 1, n_blocks - 1)

    @pl.when(jnp.logical_and(i + 1 < n_blocks, act_ref[nxt] == 1))
    def _():
        _row_dma_loop(tm, rows_of(nxt, nxt % 2), wait=False)

    @pl.when(act_ref[i] == 1)
    def _():
        _row_dma_loop(tm, rows_of(i, i % 2), wait=True)
        o_ref[...] = buf[i % 2].astype(o_ref.dtype)

    @pl.when(act_ref[i] == 0)
    def _():
        o_ref[...] = jnp.zeros(o_ref.shape, o_ref.dtype)


def moe_gather(xn, slot_tok, blk_act, *, tm):
    s = slot_tok.shape[0]
    d = xn.shape[1]
    assert tm % DMA_UNROLL == 0
    grid_spec = pltpu.PrefetchScalarGridSpec(
        num_scalar_prefetch=2,
        grid=(s // tm,),
        in_specs=[pl.BlockSpec(memory_space=pl.ANY)],
        out_specs=pl.BlockSpec((tm, d), lambda i, tok, act: (i, 0)),
        scratch_shapes=[pltpu.VMEM((2, tm, d), F32), pltpu.SemaphoreType.DMA((2,))],
    )
    return pl.pallas_call(
        functools.partial(_gather_kernel, tm=tm, n_blocks=s // tm),
        out_shape=jax.ShapeDtypeStruct((s, d), BF16),
        grid_spec=grid_spec,
        compiler_params=_params("arbitrary"),
        name="moe_gather",
    )(slot_tok, blk_act, xn)


def _combine_kernel(pos_ref, y_hbm, h_ref, gate_ref, gf_ref, o_first_ref, o_rest_ref, buf, sem,
                    *, tm, n_first, n_blocks):
    i = pl.program_id(0)

    def rows_of(blk, slot):
        def make_copy(r):
            return pltpu.make_async_copy(y_hbm.at[pl.ds(pos_ref[blk * tm * TOP_K + r], 1)],
                                         buf.at[slot, r % TOP_K, pl.ds(r // TOP_K, 1)], sem.at[slot])
        return make_copy

    @pl.when(i == 0)
    def _():
        _row_dma_loop(tm * TOP_K, rows_of(0, 0), wait=False)

    @pl.when(i + 1 < n_blocks)
    def _():
        _row_dma_loop(tm * TOP_K, rows_of(i + 1, (i + 1) % 2), wait=False)

    _row_dma_loop(tm * TOP_K, rows_of(i, i % 2), wait=True)
    gate = gate_ref[...]
    slot = i % 2
    y = buf[slot, 0] * gate[:, 0:1] + buf[slot, 1] * gate[:, 1:2]
    out = _rms(h_ref[...] + y, gf_ref[...])

    @pl.when(i < n_first)
    def _():
        o_first_ref[...] = out

    @pl.when(i >= n_first)
    def _():
        o_rest_ref[...] = out


def moe_combine(yb, pos, h, gate, g_final, *, tm, t_first):
    t, d = h.shape
    n_first = t_first // tm
    grid_spec = pltpu.PrefetchScalarGridSpec(
        num_scalar_prefetch=1,
        grid=(t // tm,),
        in_specs=[pl.BlockSpec(memory_space=pl.ANY),
                  pl.BlockSpec((tm, d), lambda i, pos: (i, 0)),
                  pl.BlockSpec((tm, LANES), lambda i, pos: (i, 0)),
                  pl.BlockSpec((1, d), lambda i, pos: (0, 0))],
        out_specs=(pl.BlockSpec((tm, d), lambda i, pos: (jnp.minimum(i, n_first - 1), 0)),
                   pl.BlockSpec((tm, d), lambda i, pos: (jnp.maximum(i - n_first, 0), 0))),
        scratch_shapes=[pltpu.VMEM((2, TOP_K, tm, d), F32), pltpu.SemaphoreType.DMA((2,))],
    )
    assert (tm * TOP_K) % DMA_UNROLL == 0 and DMA_UNROLL % TOP_K == 0
    return pl.pallas_call(
        functools.partial(_combine_kernel, tm=tm, n_first=n_first, n_blocks=t // tm),
        out_shape=(jax.ShapeDtypeStruct((t_first, d), F32), jax.ShapeDtypeStruct((t - t_first, d), F32)),
        grid_spec=grid_spec,
        compiler_params=_params("arbitrary"),
        name="moe_combine",
    )(pos, yb, h, gate, g_final.reshape(1, d))


def _ret_qk_kernel(x_ref, w_ref, cos_ref, sin_ref, o_ref, *, heads_per_step, dk):
    r = jnp.dot(x_ref[...], w_ref[...], preferred_element_type=F32)
    cos, sin = cos_ref[...], sin_ref[...]
    half = dk // 2
    for h in range(heads_per_step):
        e = r[:, h * dk:h * dk + half]
        o = r[:, h * dk + half:(h + 1) * dk]
        o_ref[:, h * dk:h * dk + half] = (e * cos - o * sin).astype(o_ref.dtype)
        o_ref[:, h * dk + half:(h + 1) * dk] = (o * cos + e * sin).astype(o_ref.dtype)


def ret_qk(xn, w_qk, cos_t, sin_t, *, dk):
    t, d = xn.shape
    n = w_qk.shape[1]
    tm = _tile(t, (640, 512, 256, 128, 64, 32, 16, 8))
    tn = _tile(n, (1024, 512, 256))
    return pl.pallas_call(
        functools.partial(_ret_qk_kernel, heads_per_step=tn // dk, dk=dk),
        out_shape=jax.ShapeDtypeStruct((t, n), BF16),
        grid=(n // tn, t // tm),
        in_specs=[pl.BlockSpec((tm, d), lambda j, i: (i, 0)), pl.BlockSpec((d, tn), lambda j, i: (0, j)),
                  pl.BlockSpec((tm, dk // 2), lambda j, i: (i, 0)),
                  pl.BlockSpec((tm, dk // 2), lambda j, i: (i, 0))],
        out_specs=pl.BlockSpec((tm, tn), lambda j, i: (i, j)),
        compiler_params=_params("parallel", "parallel"),
        name="ret_qk",
    )(xn, w_qk, cos_t, sin_t)


def _ret_kernel(q_ref, k_ref, v_ref, g_ref, s0_ref, dmat_ref, qdec_ref, kdec_ref, o_ref, s_out_ref, s_scr,
                *, nc, hps, dk, dv):
    c = pl.program_id(2)

    @pl.when(c == 0)
    def _():
        s_scr[...] = s0_ref[...]

    for h in range(hps):
        ks, vs = slice(h * dk, (h + 1) * dk), slice(h * dv, (h + 1) * dv)
        q, k, v = q_ref[:, ks], k_ref[:, ks], v_ref[:, vs]
        state = s_scr[h]
        qdec = qdec_ref[h][:, 0:1]
        kdec = kdec_ref[h][:, 0:1]
        scores = lax.dot_general(q, k, (((1,), (1,)), ((), ())), preferred_element_type=F32) * dmat_ref[h]
        intra = jnp.dot(scores.astype(BF16), v, preferred_element_type=F32)
        inter = jnp.dot(q, state.astype(BF16), preferred_element_type=F32) * qdec
        o = intra + inter
        kd = (k.astype(F32) * kdec).astype(BF16)
        chunk_decay = qdec[q.shape[0] - 1:, :]
        new_state = chunk_decay * state + lax.dot_general(kd, v, (((0,), (0,)), ((), ())),
                                                          preferred_element_type=F32)
        s_scr[h] = new_state
        g = g_ref[:, vs].astype(F32)
        o_ref[:, vs] = (g * jax.nn.sigmoid(g) * _rms(o)).astype(o_ref.dtype)

    @pl.when(c == nc - 1)
    def _():
        s_out_ref[...] = s_scr[...]


def retention(qk, v, g, s0, log_g, *, batch, heads, length, chunk, row0):
    dk, dv = s0.shape[2], s0.shape[3]
    nc = length // chunk
    blk0 = row0 // chunk
    idx = jnp.arange(chunk, dtype=F32)
    diff = idx[:, None] - idx[None, :]
    dmat = jnp.exp(jnp.where(diff[None] >= 0, diff[None] * log_g[:, None, None], -jnp.inf))
    qdec = jnp.broadcast_to(jnp.exp((idx[None, :] + 1.0) * log_g[:, None])[:, :, None], (heads, chunk, LANES))
    kdec = jnp.broadcast_to(jnp.exp((chunk - 1.0 - idx)[None, :] * log_g[:, None])[:, :, None], (heads, chunk, LANES))
    row = lambda b, h, c: blk0 + b * nc + c
    hps = _tile(heads, (4, 2, 1))
    hg = heads // hps
    return pl.pallas_call(
        functools.partial(_ret_kernel, nc=nc, hps=hps, dk=dk, dv=dv),
        out_shape=(jax.ShapeDtypeStruct((batch * length, heads * dv), BF16),
                   jax.ShapeDtypeStruct((batch, heads, dk, dv), F32)),
        grid=(batch, hg, nc),
        in_specs=[pl.BlockSpec((chunk, hps * dk), lambda b, h, c: (row(b, h, c), h)),
                  pl.BlockSpec((chunk, hps * dk), lambda b, h, c: (row(b, h, c), hg + h)),
                  pl.BlockSpec((chunk, hps * dv), lambda b, h, c: (row(b, h, c), h)),
                  pl.BlockSpec((chunk, hps * dv), lambda b, h, c: (row(b, h, c), h)),
                  pl.BlockSpec((None, hps, dk, dv), lambda b, h, c: (b, h, 0, 0)),
                  pl.BlockSpec((hps, chunk, chunk), lambda b, h, c: (h, 0, 0)),
                  pl.BlockSpec((hps, chunk, LANES), lambda b, h, c: (h, 0, 0)),
                  pl.BlockSpec((hps, chunk, LANES), lambda b, h, c: (h, 0, 0))],
        out_specs=(pl.BlockSpec((chunk, hps * dv), lambda b, h, c: (b * nc + c, h)),
                   pl.BlockSpec((None, hps, dk, dv), lambda b, h, c: (b, h, 0, 0))),
        scratch_shapes=[pltpu.VMEM((hps, dk, dv), F32)],
        compiler_params=_params("parallel", "parallel", "arbitrary"),
        name="retention",
    )(qk, qk, v, g, s0, dmat, qdec, kdec)


def _rope_half_tables(pos):
    half = MLA_ROPE // 2
    inv = 1.0 / (MLA_ROPE_BASE ** (jnp.arange(half, dtype=F32) / half))
    ang = pos.astype(F32)[:, None] * inv[None, :]
    c, s = jnp.cos(ang), jnp.sin(ang)
    z = jnp.zeros((pos.shape[0], LANES - MLA_ROPE), F32)
    return jnp.concatenate([c, c, z], axis=-1), jnp.concatenate([s, s, z], axis=-1)


def _rope_pair_tables(pos, half):
    inv = 1.0 / (RET_ROPE_BASE ** jnp.linspace(0.0, 1.0, half, dtype=F32))
    ang = pos.astype(F32)[:, None] * inv[None, :]
    return jnp.cos(ang), jnp.sin(ang)


def _rot_half_cols(w):
    half = w.shape[-1] // 2
    return jnp.concatenate([-w[..., half:], w[..., :half]], axis=-1)


def _pad_lanes(w):
    return jnp.pad(w, [(0, 0)] * (w.ndim - 1) + [(0, LANES - w.shape[-1])])


def _even_odd_cols(w, heads, dk):
    d = w.shape[0]
    return w.reshape(d, heads, dk // 2, 2).transpose(0, 1, 3, 2).reshape(d, heads * dk)


def _even_odd_rows(s, inverse=False):
    b, h, dk, dv = s.shape
    if inverse:
        return s.reshape(b, h, 2, dk // 2, dv).transpose(0, 1, 3, 2, 4).reshape(b, h, dk, dv)
    return s.reshape(b, h, dk // 2, 2, dv).transpose(0, 1, 3, 2, 4).reshape(b, h, dk, dv)


def _last_active(act):
    return lax.cummax(jnp.where(act > 0, jnp.arange(act.shape[0], dtype=jnp.int32), 0), axis=0)


def _route(idx, n_tok):
    n_assign = n_tok * TOP_K
    flat_e = idx.reshape(-1)
    onehot = (flat_e[:, None] == jnp.arange(N_EXPERTS, dtype=jnp.int32)[None, :]).astype(jnp.int32)
    csum = jnp.cumsum(onehot, axis=0)
    rank = jnp.sum((csum - onehot) * onehot, axis=1)
    counts = csum[-1]
    padded = (counts + MOE_SUPER - 1) // MOE_SUPER * MOE_SUPER
    pad_end = jnp.cumsum(padded)
    pad_start = pad_end - padded
    dest = (jnp.sum(pad_start[None, :] * onehot, axis=1) + rank).astype(jnp.int32)
    n_super = (n_assign + N_EXPERTS * (MOE_SUPER - 1)) // MOE_SUPER
    per_super = MOE_SUPER // MOE_BLOCK
    flat_tok = jnp.arange(n_assign, dtype=jnp.int32) // TOP_K
    slot_tok = jnp.zeros((n_super * MOE_SUPER,), jnp.int32).at[dest].set(flat_tok)
    sup_start = jnp.arange(n_super, dtype=jnp.int32) * MOE_SUPER
    sup_e = jnp.minimum(jnp.sum(sup_start[:, None] >= pad_end[None, :], axis=1), N_EXPERTS - 1).astype(jnp.int32)
    used = jnp.clip(counts[sup_e] - (sup_start - pad_start[sup_e]), 0, MOE_SUPER)
    sup_nact = ((used + MOE_BLOCK - 1) // MOE_BLOCK).astype(jnp.int32)
    blk_e = jnp.repeat(sup_e, per_super)
    blk_act = (jnp.arange(per_super, dtype=jnp.int32)[None, :] < sup_nact[:, None]).astype(jnp.int32).reshape(-1)
    return slot_tok, dest, (sup_e, sup_nact, _last_active(sup_nact)), (blk_e, blk_act, _last_active(blk_act))


def kernel(x_prompt, x_sample, mem_prompt, cache_mla_ckv, cache_mla_krope, state_ret, cache_mem_k, cache_mem_v, norm_mix, norm_xattn, norm_ffn, norm_final, mla_w_dq, mla_g_q, mla_w_uq, mla_w_dkv, mla_g_kv, mla_w_uk, mla_w_uv, mla_w_o, ret_w_q, ret_w_k, ret_w_v, ret_w_g, ret_w_o, xa_w_q, xa_w_k, xa_w_v, xa_w_o, ffn_w1, ffn_w3, ffn_w2, moe_w_router, moe_w1, moe_w3, moe_w2):
    bp, lp, d = x_prompt.shape
    bs, ls, _ = x_sample.shape
    past = cache_mla_ckv.shape[2]
    n_mem = mem_prompt.shape[1]
    tp, ts = bp * lp, bs * ls
    t = tp + ts
    heads = mla_w_uk.shape[2]
    q_rank, kv_rank = mla_w_dq.shape[2], mla_w_uk.shape[1]
    dk = d // RET_HEADS
    dv = 2 * dk
    xa_dim = XA_HEADS * XA_HEAD_DIM

    pos = jnp.concatenate([jnp.tile(jnp.arange(lp, dtype=jnp.int32), bp),
                           jnp.tile(past + jnp.arange(ls, dtype=jnp.int32), bs)])
    x_p, x_s = x_prompt.reshape(tp, d), x_sample.reshape(ts, d)
    tm_x = _tile(lp, (128, 64, 32))

    def cross_attention(h_p, h_s, layer, k_p, v_p, xn_dtype):
        wq, wo = xa_w_q[layer].astype(BF16), xa_w_o[layer].astype(BF16)
        k_all = jnp.concatenate([k_p, cache_mem_k[layer].reshape(bs * n_mem, xa_dim)], axis=0).astype(BF16)
        v_all = jnp.concatenate([v_p, cache_mem_v[layer].reshape(bs * n_mem, xa_dim)], axis=0).astype(BF16)
        return xattn_block(h_p, h_s, norm_xattn[layer], wq, k_all, v_all, wo, norm_ffn[layer], n_long=bp,
                           len_long=lp, len_short=ls, n_mem=n_mem, tm=tm_x, xn_dtype=xn_dtype)

    def memory_kv(layer):
        w = jnp.concatenate([xa_w_k[layer], xa_w_v[layer]], axis=1).astype(BF16)
        kv = matmul(mem_prompt.reshape(bp * n_mem, d).astype(BF16), w, out_dtype=F32, name="mem_kv")
        return kv[:, :xa_dim], kv[:, xa_dim:]

    xn = rmsnorm((x_p, x_s), norm_mix[0], BF16)
    cos_m, sin_m = _rope_half_tables(pos)
    w_dkv = mla_w_dkv[0]
    w_kr = w_dkv[:, kv_rank:]
    w_pre = jnp.concatenate([mla_w_dq[0], w_dkv[:, :kv_rank], _pad_lanes(w_kr), _pad_lanes(_rot_half_cols(w_kr))],
                            axis=1).astype(BF16)
    cq, ckv, ckv_b, kr, kr_b = mla_pre(xn, w_pre, mla_g_q[0], mla_g_kv[0], cos_m, sin_m)

    w_uq = mla_w_uq[0].reshape(q_rank, heads, MLA_NOPE + MLA_ROPE)
    wn = w_uq[:, :, :MLA_NOPE].reshape(q_rank, heads * LANES).astype(BF16)
    w_rope = w_uq[:, :, MLA_NOPE:]
    wr = jnp.concatenate([w_rope, _rot_half_cols(w_rope)], axis=-1).reshape(q_rank, heads * LANES).astype(BF16)
    cs_m = jnp.concatenate([cos_m[:, :MLA_ROPE], sin_m[:, :MLA_ROPE]], axis=-1)
    q = mla_q(cq, wn, wr, cs_m, scale=float((MLA_NOPE + MLA_ROPE) ** -0.5 * 1.4426950408889634))

    w_ukv = jnp.concatenate([mla_w_uk[0].reshape(kv_rank, heads * MLA_NOPE),
                             mla_w_uv[0].reshape(kv_rank, heads * MLA_V)], axis=1).astype(BF16)
    kv = matmul(ckv_b, w_ukv, out_dtype=BF16, name="mla_kv")
    t_att = _tile(lp, (1024, 512, 256, 128, 64))
    o_p = flash_attention(q, kv, kr_b, batch=bp, heads=heads, lq=lp, lk=lp, lk_valid=lp, q_off=0,
                          tq=t_att, tk=t_att)

    lk_s = past + ls
    lk_pad = -(-lk_s // LANES) * LANES
    ckv_all = jnp.concatenate([cache_mla_ckv[0].astype(BF16), ckv_b[tp:].reshape(bs, ls, kv_rank),
                               jnp.zeros((bs, lk_pad - lk_s, kv_rank), BF16)], axis=1)
    kr_all = jnp.concatenate([_pad_lanes(cache_mla_krope[0]).astype(BF16), kr_b[tp:].reshape(bs, ls, LANES),
                              jnp.zeros((bs, lk_pad - lk_s, LANES), BF16)], axis=1)
    kv_s = matmul(ckv_all.reshape(bs * lk_pad, kv_rank), w_ukv, out_dtype=BF16, name="mla_kv_sample")
    o_s = flash_attention(q[tp:], kv_s, kr_all.reshape(bs * lk_pad, LANES), batch=bs, heads=heads, lq=ls,
                          lk=lk_pad, lk_valid=lk_s, q_off=past, tq=ls, tk=lk_pad)
    w_o = mla_w_o[0].astype(BF16)
    h1_p = matmul(o_p, w_o, out_dtype=F32, res=x_p, name="mla_out")
    h1_s = matmul(o_s, w_o, out_dtype=F32, res=x_s, name="mla_out_sample")

    mk0, mv0 = memory_kv(0)
    h2, xn_f = cross_attention(h1_p, h1_s, 0, mk0, mv0, BF16)

    d_ff = ffn_w1.shape[2]
    hf = dense_gateup(xn_f, ffn_w1[0].astype(BF16), ffn_w3[0].astype(BF16),
                      tm=_tile(t, (1280, 640, 256, 128, 64, 32)), tf=_tile(d_ff, (512, 256, 128)))
    h3 = matmul(hf, ffn_w2[0].astype(BF16), out_dtype=F32, res=h2, name="ffn_down")

    xn = rmsnorm(h3, norm_mix[1], BF16)
    cos_r, sin_r = _rope_pair_tables(pos, dk // 2)
    w_qk = jnp.concatenate([_even_odd_cols(ret_w_q[0], RET_HEADS, dk),
                            _even_odd_cols(ret_w_k[0] * (dk ** -0.5), RET_HEADS, dk)], axis=1).astype(BF16)
    qk = ret_qk(xn, w_qk, cos_r, sin_r, dk=dk)
    rv = matmul(xn, ret_w_v[0].astype(BF16), out_dtype=BF16, name="ret_v")
    rg = matmul(xn, ret_w_g[0].astype(BF16), out_dtype=BF16, name="ret_g")
    log_g = jnp.log1p(-jnp.exp2(-5.0 - jnp.arange(RET_HEADS, dtype=F32)))
    ch_p = _tile(lp, (256, 128, 64))
    og_p, st_p = retention(qk, rv, rg, jnp.zeros((bp, RET_HEADS, dk, dv), F32), log_g, batch=bp, heads=RET_HEADS,
                           length=lp, chunk=ch_p, row0=0)
    og_s, st_s = retention(qk, rv, rg, _even_odd_rows(state_ret[0]), log_g, batch=bs, heads=RET_HEADS,
                           length=ls, chunk=ls, row0=tp)
    w_o = ret_w_o[0].astype(BF16)
    h4_p = matmul(og_p, w_o, out_dtype=F32, res=h3, name="ret_out")
    h4_s = matmul(og_s, w_o, out_dtype=F32, res=h3, res_row0=tp, name="ret_out_sample")

    mk1, mv1 = memory_kv(1)
    h5, xn_f = cross_attention(h4_p, h4_s, 1, mk1, mv1, F32)

    idx_l, gate_l = moe_router(xn_f, moe_w_router[0])
    slot_tok, dest, sup, blk = _route(idx_l[:, :TOP_K], t)
    xs = moe_gather(xn_f, slot_tok, blk[1], tm=MOE_BLOCK)
    hs = moe_gateup(xs, moe_w1[0], moe_w3[0], *sup, tm=MOE_SUPER, tf=_tile(d_ff, (512, 256, 128)))
    yb = moe_down(hs, moe_w2[0].astype(BF16), *blk, tm=MOE_BLOCK, tn=_tile(d, (512, 256, 128)))
    y_p, y_s = moe_combine(yb, dest, h5, gate_l, norm_final, tm=_tile(ts, (128, 64, 32, 16, 8)), t_first=tp)

    def split(a, width):
        return a[:tp].reshape(1, bp, lp, width), a[tp:].reshape(1, bs, ls, width)

    ckv_p, ckv_s = split(ckv, kv_rank)
    kr_p, kr_s = split(kr, MLA_ROPE)
    mem_shape = (bp, n_mem, XA_HEADS, XA_HEAD_DIM)
    return (y_p.reshape(bp, lp, d), y_s.reshape(bs, ls, d),
            ckv_p, kr_p, _even_odd_rows(st_p, inverse=True)[None],
            jnp.stack([mk0.reshape(mem_shape), mk1.reshape(mem_shape)]),
            jnp.stack([mv0.reshape(mem_shape), mv1.reshape(mem_shape)]),
            ckv_s, kr_s, _even_odd_rows(st_s, inverse=True)[None])
```

```python
import functools

import jax
import jax.numpy as jnp
from jax import lax
from jax.experimental import pallas as pl
from jax.experimental.pallas import tpu as pltpu

F32 = jnp.float32
BF16 = jnp.bfloat16

RMS_EPS = 1e-6
CHUNK = 64
MLA_NOPE = 128
MLA_ROPE = 64
MLA_V = 128
MLA_ROPE_BASE = 10000.0
RET_HEADS = 16
RET_ROPE_BASE = 10000.0
XA_HEADS = 4
XA_HEAD_DIM = 128
N_EXPERTS = 8
TOP_K = 2
MOE_BLOCK = 256
MOE_SUPER = 1024
DMA_UNROLL = 8
LANES = 128
NEG_BIG = -1e30
VMEM_LIMIT_BYTES = 56 * 1024 * 1024


def _params(*sem):
    return pltpu.CompilerParams(dimension_semantics=sem, vmem_limit_bytes=VMEM_LIMIT_BYTES)


def _tile(n, candidates):
    for c in candidates:
        if c <= n and n % c == 0:
            return c
    return n


def _rms(x, g=None):
    y = x * lax.rsqrt(jnp.mean(x * x, axis=-1, keepdims=True) + RMS_EPS)
    return y if g is None else y * g


def _two_part_specs(tm, d, n_first):
    return [pl.BlockSpec((tm, d), lambda i: (jnp.minimum(i, n_first - 1), 0)),
            pl.BlockSpec((tm, d), lambda i: (jnp.maximum(i - n_first, 0), 0))]


def _two_part_tile(first_ref, rest_ref, n_first):
    return jnp.where(pl.program_id(0) < n_first, first_ref[...], rest_ref[...])


def _rmsnorm_kernel(x_ref, g_ref, o_ref):
    o_ref[...] = _rms(x_ref[...], g_ref[...]).astype(o_ref.dtype)


def _rmsnorm2_kernel(xa_ref, xb_ref, g_ref, o_ref, *, n_first):
    o_ref[...] = _rms(_two_part_tile(xa_ref, xb_ref, n_first), g_ref[...]).astype(o_ref.dtype)


def rmsnorm(x, g, out_dtype):
    first, rest = x if isinstance(x, tuple) else (x, None)
    d = first.shape[1]
    t = first.shape[0] + (0 if rest is None else rest.shape[0])
    tm = _tile(t if rest is None else rest.shape[0], (256, 128, 64, 32, 16, 8))
    row = pl.BlockSpec((tm, d), lambda i: (i, 0))
    gain = pl.BlockSpec((1, d), lambda i: (0, 0))
    if rest is None:
        body, in_specs, args = _rmsnorm_kernel, [row, gain], (first,)
    else:
        n_first = first.shape[0] // tm
        body = functools.partial(_rmsnorm2_kernel, n_first=n_first)
        in_specs, args = _two_part_specs(tm, d, n_first) + [gain], (first, rest)
    return pl.pallas_call(
        body,
        out_shape=jax.ShapeDtypeStruct((t, d), out_dtype),
        grid=(t // tm,),
        in_specs=in_specs,
        out_specs=row,
        compiler_params=_params("parallel"),
        name="rmsnorm",
    )(*args, g.reshape(1, d))


def _mm_kernel(*refs, nk, has_res):
    a_ref, w_ref = refs[0], refs[1]
    res_ref = refs[2] if has_res else None
    o_ref = refs[2 + has_res]
    part = jnp.dot(a_ref[...], w_ref[...], preferred_element_type=F32)
    if nk == 1:
        if has_res:
            part = part + res_ref[...]
        o_ref[...] = part.astype(o_ref.dtype)
        return
    acc_ref = refs[3 + has_res]
    k = pl.program_id(2)

    @pl.when(k == 0)
    def _():
        acc_ref[...] = part

    @pl.when(k > 0)
    def _():
        acc_ref[...] += part

    @pl.when(k == nk - 1)
    def _():
        out = acc_ref[...]
        if has_res:
            out = out + res_ref[...]
        o_ref[...] = out.astype(o_ref.dtype)


def matmul(a, w, *, out_dtype, res=None, res_row0=0, tm=None, tn=None, tk=None, name="matmul"):
    m, kdim = a.shape
    n = w.shape[1]
    tk = tk or (kdim if kdim <= 4096 else _tile(kdim, (2048, 1024, 512)))
    row_tiles = (1280, 1024) if tk <= 2048 else ()
    tm = tm or _tile(m, row_tiles + (640, 512, 256, 128, 64, 32, 16, 8))
    tn = tn or _tile(n, (1024, 512, 256, 128))
    nk = kdim // tk
    has_res = res is not None
    in_specs = [pl.BlockSpec((tm, tk), lambda j, i, k: (i, k)),
                pl.BlockSpec((tk, tn), lambda j, i, k: (k, j))]
    args = [a, w]
    if has_res:
        assert res_row0 % tm == 0
        in_specs.append(pl.BlockSpec((tm, tn), lambda j, i, k: (i + res_row0 // tm, j)))
        args.append(res)
    scratch = [pltpu.VMEM((tm, tn), F32)] if nk > 1 else []
    return pl.pallas_call(
        functools.partial(_mm_kernel, nk=nk, has_res=has_res),
        out_shape=jax.ShapeDtypeStruct((m, n), out_dtype),
        grid=(n // tn, m // tm, nk),
        in_specs=in_specs,
        out_specs=pl.BlockSpec((tm, tn), lambda j, i, k: (i, j)),
        scratch_shapes=scratch,
        compiler_params=_params("parallel", "parallel", "arbitrary"),
        name=name,
    )(*args)


def _mla_pre_kernel(x_ref, w_ref, gq_ref, gkv_ref, cos_ref, sin_ref,
                    cq_ref, ckv_ref, ckvb_ref, kr_ref, krb_ref, *, q_rank, kv_rank):
    r = jnp.dot(x_ref[...], w_ref[...], preferred_element_type=F32)
    cq_ref[...] = _rms(r[:, :q_rank], gq_ref[...]).astype(cq_ref.dtype)
    ckv = _rms(r[:, q_rank:q_rank + kv_rank], gkv_ref[...])
    ckv_ref[...] = ckv
    ckvb_ref[...] = ckv.astype(ckvb_ref.dtype)
    o = q_rank + kv_rank
    kr = r[:, o:o + LANES] * cos_ref[...] + r[:, o + LANES:o + 2 * LANES] * sin_ref[...]
    kr_ref[...] = kr[:, :MLA_ROPE]
    krb_ref[...] = kr.astype(krb_ref.dtype)


def mla_pre(xn, w_pre, g_q, g_kv, cos_t, sin_t):
    t, d = xn.shape
    q_rank, kv_rank = g_q.shape[0], g_kv.shape[0]
    n = w_pre.shape[1]
    tm = _tile(t, (256, 128, 64, 32, 16, 8))
    row = lambda i: (i, 0)
    fixed = lambda i: (0, 0)
    return pl.pallas_call(
        functools.partial(_mla_pre_kernel, q_rank=q_rank, kv_rank=kv_rank),
        out_shape=(jax.ShapeDtypeStruct((t, q_rank), BF16),
                   jax.ShapeDtypeStruct((t, kv_rank), F32),
                   jax.ShapeDtypeStruct((t, kv_rank), BF16),
                   jax.ShapeDtypeStruct((t, MLA_ROPE), F32),
                   jax.ShapeDtypeStruct((t, LANES), BF16)),
        grid=(t // tm,),
        in_specs=[pl.BlockSpec((tm, d), row), pl.BlockSpec((d, n), fixed),
                  pl.BlockSpec((1, q_rank), fixed), pl.BlockSpec((1, kv_rank), fixed),
                  pl.BlockSpec((tm, LANES), row), pl.BlockSpec((tm, LANES), row)],
        out_specs=(pl.BlockSpec((tm, q_rank), row), pl.BlockSpec((tm, kv_rank), row),
                   pl.BlockSpec((tm, kv_rank), row), pl.BlockSpec((tm, MLA_ROPE), row),
                   pl.BlockSpec((tm, LANES), row)),
        compiler_params=_params("parallel"),
        name="mla_pre",
    )(xn, w_pre, g_q.reshape(1, -1), g_kv.reshape(1, -1), cos_t, sin_t)


def _mla_q_kernel(c_ref, wn_ref, wr_ref, cs_ref, q_ref, *, heads_per_step, scale):
    c = c_ref[...]
    qn = jnp.dot(c, wn_ref[...], preferred_element_type=F32) * scale
    qr = jnp.dot(c, wr_ref[...], preferred_element_type=F32)
    cs = cs_ref[...] * scale
    for h in range(heads_per_step):
        lo, hi = h * LANES, (h + 1) * LANES
        q_ref[:, 2 * lo:2 * lo + LANES] = qn[:, lo:hi].astype(q_ref.dtype)
        terms = qr[:, lo:hi] * cs
        q_ref[:, 2 * lo + LANES:2 * hi] = (terms + pltpu.roll(terms, LANES // 2, 1)).astype(q_ref.dtype)


def mla_q(cq, wn, wr, cs_t, *, scale):
    t, q_rank = cq.shape
    n = wn.shape[1]
    tm = _tile(t, (1280, 1024, 640, 512, 256, 128, 64, 32, 16, 8))
    tn = _tile(n, (512, 256, 128))
    w_spec = pl.BlockSpec((q_rank, tn), lambda j, i: (0, j))
    return pl.pallas_call(
        functools.partial(_mla_q_kernel, heads_per_step=tn // LANES, scale=scale),
        out_shape=jax.ShapeDtypeStruct((t, 2 * n), BF16),
        grid=(n // tn, t // tm),
        in_specs=[pl.BlockSpec((tm, q_rank), lambda j, i: (i, 0)), w_spec, w_spec,
                  pl.BlockSpec((tm, LANES), lambda j, i: (i, 0))],
        out_specs=pl.BlockSpec((tm, 2 * tn), lambda j, i: (i, j)),
        compiler_params=_params("parallel", "parallel"),
        name="mla_q",
    )(cq, wn, wr, cs_t)


def _flash_last_k(qi, *, tq, tk, nk, q_off):
    last_pos = ((q_off + (qi + 1) * tq - 1) // CHUNK) * CHUNK + CHUNK - 1
    return min(nk - 1, last_pos // tk)


def _flash_kernel(qi_ref, ki_ref, q_ref, kn_ref, kr_ref, v_ref, o_ref, m_scr, l_scr, acc_scr,
                  *, tq, tk, rb, hp, nk, q_off, lk_valid, aligned):
    step = pl.program_id(2)
    qi, ki = qi_ref[step], ki_ref[step]
    first_q = q_off + qi * tq
    last_seen_by_first_row = (first_q // CHUNK) * CHUNK + CHUNK - 1
    tile_last_key = ki * tk + tk - 1
    needs_mask = jnp.logical_or(tile_last_key > last_seen_by_first_row, tile_last_key >= lk_valid)
    last_ki = jnp.minimum(nk - 1, (((first_q + tq - 1) // CHUNK) * CHUNK + CHUNK - 1) // tk)
    shift = CHUNK.bit_length() - 1

    @pl.when(ki == 0)
    def _():
        m_scr[...] = jnp.full(m_scr.shape, NEG_BIG, F32)
        l_scr[...] = jnp.zeros(l_scr.shape, F32)
        acc_scr[...] = jnp.zeros(acc_scr.shape, F32)

    def attend(masked):
        block = rb if (masked and aligned) else tq
        for h in range(hp):
            hl = slice(h * LANES, (h + 1) * LANES)
            for r in range(tq // block):
                rows = slice(r * block, (r + 1) * block)
                nc = min(tk, (r + 1) * block) if (masked and aligned) else tk
                k = jnp.concatenate([kn_ref[:nc, hl], kr_ref[:nc, :]], axis=-1)
                s = lax.dot_general(q_ref[rows, 2 * h * LANES:2 * (h + 1) * LANES], k, (((1,), (1,)), ((), ())),
                                    preferred_element_type=F32)
                if masked:
                    qpos = first_q + r * block + lax.broadcasted_iota(jnp.int32, (block, nc), 0)
                    kpos = ki * tk + lax.broadcasted_iota(jnp.int32, (block, nc), 1)
                    s = jnp.where(lax.shift_right_arithmetic(kpos, shift) <= lax.shift_right_arithmetic(qpos, shift),
                                  s, NEG_BIG)
                    if lk_valid < nk * tk:
                        s = jnp.where(kpos < lk_valid, s, NEG_BIG)
                m_prev = m_scr[h, rows, :]
                m_new = jnp.maximum(m_prev, jnp.max(s, axis=-1, keepdims=True))
                alpha = jnp.exp2(m_prev - m_new)
                p = jnp.exp2(s - jnp.tile(m_new, (1, nc // LANES)))
                v_ones = jnp.concatenate([v_ref[:nc, hl], jnp.ones((nc, LANES), BF16)], axis=-1)
                pv = jnp.dot(p.astype(BF16), v_ones, preferred_element_type=F32)
                l_scr[h, rows, :] = alpha * l_scr[h, rows, :] + pv[:, LANES:]
                acc_scr[rows, hl] = alpha * acc_scr[rows, hl] + pv[:, :LANES]
                m_scr[h, rows, :] = m_new

    pl.when(needs_mask)(functools.partial(attend, True))
    pl.when(jnp.logical_not(needs_mask))(functools.partial(attend, False))

    @pl.when(ki == last_ki)
    def _():
        for h in range(hp):
            hl = slice(h * LANES, (h + 1) * LANES)
            o_ref[:, hl] = (acc_scr[:, hl] / l_scr[h]).astype(o_ref.dtype)


def flash_attention(q, kv, kr, *, batch, heads, lq, lk, lk_valid, q_off, tq, tk):
    assert CHUNK & (CHUNK - 1) == 0 and MLA_V == LANES and MLA_NOPE == LANES
    nq, nk = lq // tq, lk // tk
    pairs = [(qi, ki) for qi in range(nq)
             for ki in range(_flash_last_k(qi, tq=tq, tk=tk, nk=nk, q_off=q_off) + 1)]
    qi_of = jnp.asarray([p[0] for p in pairs], jnp.int32)
    ki_of = jnp.asarray([p[1] for p in pairs], jnp.int32)
    hp = _tile(heads, (4, 2, 1))
    hg = heads // hp
    rb = _tile(tq, (256, 128, 64, 32, 16))
    aligned = q_off % tq == 0 and tq == tk and lk_valid == lk and rb % max(CHUNK, LANES) == 0
    grid_spec = pltpu.PrefetchScalarGridSpec(
        num_scalar_prefetch=2,
        grid=(batch, hg, len(pairs)),
        in_specs=[pl.BlockSpec((tq, 2 * hp * LANES), lambda b, h, s, qi, ki: (b * nq + qi[s], h)),
                  pl.BlockSpec((tk, hp * LANES), lambda b, h, s, qi, ki: (b * nk + ki[s], h)),
                  pl.BlockSpec((tk, LANES), lambda b, h, s, qi, ki: (b * nk + ki[s], 0)),
                  pl.BlockSpec((tk, hp * LANES), lambda b, h, s, qi, ki: (b * nk + ki[s], hg + h))],
        out_specs=pl.BlockSpec((tq, hp * LANES), lambda b, h, s, qi, ki: (b * nq + qi[s], h)),
        scratch_shapes=[pltpu.VMEM((hp, tq, LANES), F32), pltpu.VMEM((hp, tq, LANES), F32),
                        pltpu.VMEM((tq, hp * LANES), F32)],
    )
    return pl.pallas_call(
        functools.partial(_flash_kernel, tq=tq, tk=tk, rb=rb, hp=hp, nk=nk, q_off=q_off, lk_valid=lk_valid,
                          aligned=aligned),
        out_shape=jax.ShapeDtypeStruct((batch * lq, heads * MLA_V), BF16),
        grid_spec=grid_spec,
        compiler_params=_params("parallel", "parallel", "arbitrary"),
        name="mla_flash",
    )(qi_of, ki_of, q, kv, kr, kv)


def _xattn_attend(q, k, v):
    scale = float(XA_HEAD_DIM ** -0.5)
    outs = []
    for hd in range(XA_HEADS):
        sl = slice(hd * XA_HEAD_DIM, (hd + 1) * XA_HEAD_DIM)
        s = lax.dot_general(q[:, sl], k[:, sl], (((1,), (1,)), ((), ())), preferred_element_type=F32) * scale
        p = jnp.exp(s - jnp.max(s, axis=-1, keepdims=True))
        p = p / jnp.sum(p, axis=-1, keepdims=True)
        outs.append(jnp.dot(p.astype(BF16), v[:, sl], preferred_element_type=F32).astype(BF16))
    return jnp.concatenate(outs, axis=-1)


def _xattn_kernel(*refs, sub, n_long_tiles):
    ha_ref, hb_ref, gx_ref, wq_ref, wo_ref, gf_ref = refs[:6]
    refs = refs[1:]
    k_refs, v_refs = refs[5:5 + sub], refs[5 + sub:5 + 2 * sub]
    h_out, xn_out, o_scr = refs[5 + 2 * sub:]
    i = pl.program_id(0)
    x = _two_part_tile(ha_ref, hb_ref, n_long_tiles)
    xn = _rms(x, gx_ref[...]).astype(BF16)
    q = jnp.dot(xn, wq_ref[...], preferred_element_type=F32).astype(BF16)
    rows = q.shape[0] // sub

    @pl.when(i < n_long_tiles)
    def _():
        o_scr[...] = _xattn_attend(q, k_refs[0][...], v_refs[0][...])

    @pl.when(i >= n_long_tiles)
    def _():
        for j in range(sub):
            o_scr[j * rows:(j + 1) * rows, :] = _xattn_attend(q[j * rows:(j + 1) * rows], k_refs[j][...], v_refs[j][...])

    h_new = x + jnp.dot(o_scr[...], wo_ref[...], preferred_element_type=F32)
    h_out[...] = h_new
    xn_out[...] = _rms(h_new, gf_ref[...]).astype(xn_out.dtype)


def xattn_block(h_long, h_short, g_x, w_q, k, v, w_o, g_f, *, n_long, len_long, len_short, n_mem, tm, xn_dtype):
    d = h_long.shape[1]
    t = h_long.shape[0] + h_short.shape[0]
    hd = XA_HEADS * XA_HEAD_DIM
    sub = tm // len_short
    tiles_per_long = len_long // tm
    n_long_tiles = n_long * tiles_per_long
    row = lambda i: (i, 0)
    fixed = lambda i: (0, 0)

    def mem(j):
        def index(i):
            short = n_long + (i - n_long_tiles) * sub + j
            return (jnp.where(i < n_long_tiles, i // tiles_per_long, short), 0)
        return pl.BlockSpec((n_mem, hd), index)

    in_specs = (_two_part_specs(tm, d, n_long_tiles)
                + [pl.BlockSpec((1, d), fixed), pl.BlockSpec((d, hd), fixed),
                   pl.BlockSpec((hd, d), fixed), pl.BlockSpec((1, d), fixed)]
                + [mem(j) for j in range(sub)] * 2)
    return pl.pallas_call(
        functools.partial(_xattn_kernel, sub=sub, n_long_tiles=n_long_tiles),
        out_shape=(jax.ShapeDtypeStruct((t, d), F32), jax.ShapeDtypeStruct((t, d), xn_dtype)),
        grid=(t // tm,),
        in_specs=in_specs,
        out_specs=(pl.BlockSpec((tm, d), row), pl.BlockSpec((tm, d), row)),
        scratch_shapes=[pltpu.VMEM((tm, hd), BF16)],
        compiler_params=_params("parallel"),
        name="xattn",
    )(h_long, h_short, g_x.reshape(1, d), w_q, w_o, g_f.reshape(1, d), *([k] * sub), *([v] * sub))


def _swiglu(x, w1, w3):
    a = jnp.dot(x, w1, preferred_element_type=F32)
    b = jnp.dot(x, w3, preferred_element_type=F32)
    return (a * jax.nn.sigmoid(a) * b).astype(BF16)


def _dense_gateup_kernel(x_ref, w1_ref, w3_ref, o_ref):
    o_ref[...] = _swiglu(x_ref[...], w1_ref[...], w3_ref[...])


def dense_gateup(x, w1, w3, *, tm, tf):
    t, d = x.shape
    f = w1.shape[1]
    w_spec = pl.BlockSpec((d, tf), lambda j, i: (0, j))
    return pl.pallas_call(
        _dense_gateup_kernel,
        out_shape=jax.ShapeDtypeStruct((t, f), BF16),
        grid=(f // tf, t // tm),
        in_specs=[pl.BlockSpec((tm, d), lambda j, i: (i, 0)), w_spec, w_spec],
        out_specs=pl.BlockSpec((tm, tf), lambda j, i: (i, j)),
        compiler_params=_params("parallel", "parallel"),
        name="dense_gateup",
    )(x, w1, w3)


def _moe_gateup_kernel(se_ref, nact_ref, src_ref, x_ref, w1_ref, w3_ref, o_ref, *, sub, n_sub):
    nact = nact_ref[pl.program_id(0)]

    @pl.when(nact == n_sub)
    def _():
        o_ref[...] = _swiglu(x_ref[...], w1_ref[...].astype(BF16), w3_ref[...].astype(BF16))

    @pl.when(nact < n_sub)
    def _():
        for b in range(n_sub):
            rows = slice(b * sub, (b + 1) * sub)

            @pl.when(b < nact)
            def _():
                o_ref[rows, :] = _swiglu(x_ref[rows, :], w1_ref[...].astype(BF16), w3_ref[...].astype(BF16))

            @pl.when(b >= nact)
            def _():
                o_ref[rows, :] = jnp.zeros((sub, o_ref.shape[1]), o_ref.dtype)


def moe_gateup(xs, w1, w3, sup_e, sup_nact, sup_src, *, tm, tf):
    s, d = xs.shape
    f = w1.shape[2]
    w_spec = pl.BlockSpec((None, d, tf), lambda i, j, se, na, src: (se[i], 0, jnp.where(na[i] > 0, j, 0)))
    grid_spec = pltpu.PrefetchScalarGridSpec(
        num_scalar_prefetch=3,
        grid=(s // tm, f // tf),
        in_specs=[pl.BlockSpec((tm, d), lambda i, j, se, na, src: (src[i], 0), pipeline_mode=pl.Buffered(1)),
                  w_spec, w_spec],
        out_specs=pl.BlockSpec((tm, tf), lambda i, j, se, na, src: (i, j)),
    )
    return pl.pallas_call(
        functools.partial(_moe_gateup_kernel, sub=MOE_BLOCK, n_sub=tm // MOE_BLOCK),
        out_shape=jax.ShapeDtypeStruct((s, f), BF16),
        grid_spec=grid_spec,
        compiler_params=_params("parallel", "arbitrary"),
        name="moe_gateup",
    )(sup_e, sup_nact, sup_src, xs, w1, w3)


def _moe_down_kernel(be_ref, act_ref, src_ref, h_ref, w_ref, o_ref):
    i = pl.program_id(1)

    @pl.when(act_ref[i] == 1)
    def _():
        o_ref[...] = jnp.dot(h_ref[...], w_ref[...], preferred_element_type=F32)

    @pl.when(act_ref[i] == 0)
    def _():
        o_ref[...] = jnp.zeros(o_ref.shape, o_ref.dtype)


def moe_down(hs, w2, blk_e, blk_act, blk_src, *, tm, tn):
    s, f = hs.shape
    d = w2.shape[2]
    grid_spec = pltpu.PrefetchScalarGridSpec(
        num_scalar_prefetch=3,
        grid=(d // tn, s // tm),
        in_specs=[pl.BlockSpec((tm, f), lambda j, i, be, act, src: (src[i], 0)),
                  pl.BlockSpec((None, f, tn), lambda j, i, be, act, src: (be[i], 0, j))],
        out_specs=pl.BlockSpec((tm, tn), lambda j, i, be, act, src: (i, j)),
    )
    return pl.pallas_call(
        _moe_down_kernel,
        out_shape=jax.ShapeDtypeStruct((s, d), F32),
        grid_spec=grid_spec,
        compiler_params=_params("parallel", "arbitrary"),
        name="moe_down",
    )(blk_e, blk_act, blk_src, hs, w2)


def _router_kernel(x_ref, wh_ref, wl_ref, idx_ref, gate_ref):
    x = x_ref[...]
    xh = x.astype(BF16)
    xl = (x - xh.astype(F32)).astype(BF16)
    wh, wl = wh_ref[...], wl_ref[...]
    logits = (jnp.dot(xh, wh, preferred_element_type=F32) + jnp.dot(xl, wh, preferred_element_type=F32)
              + jnp.dot(xh, wl, preferred_element_type=F32))
    lane = lax.broadcasted_iota(jnp.int32, logits.shape, 1)
    logits = jnp.where(lane < N_EXPERTS, logits, -jnp.inf)
    m1 = jnp.max(logits, axis=-1, keepdims=True)
    i1 = jnp.min(jnp.where(logits == m1, lane, LANES), axis=-1, keepdims=True)
    rest = jnp.where(lane == i1, -jnp.inf, logits)
    m2 = jnp.max(rest, axis=-1, keepdims=True)
    i2 = jnp.min(jnp.where(rest == m2, lane, LANES), axis=-1, keepdims=True)
    e = jnp.exp(m2 - m1)
    g1 = 1.0 / (1.0 + e)
    g2 = e / (1.0 + e)
    idx_ref[...] = jnp.where(lane == 0, i1, jnp.where(lane == 1, i2, 0))
    gate_ref[...] = jnp.where(lane == 0, g1, jnp.where(lane == 1, g2, 0.0))


def moe_router(xn, w_router):
    t, d = xn.shape
    w = jnp.zeros((d, LANES), F32).at[:, :N_EXPERTS].set(w_router)
    wh = w.astype(BF16)
    wl = (w - wh.astype(F32)).astype(BF16)
    tm = _tile(t, (256, 128, 64, 32, 16, 8))
    row = lambda i: (i, 0)
    fixed = lambda i: (0, 0)
    return pl.pallas_call(
        _router_kernel,
        out_shape=(jax.ShapeDtypeStruct((t, LANES), jnp.int32), jax.ShapeDtypeStruct((t, LANES), F32)),
        grid=(t // tm,),
        in_specs=[pl.BlockSpec((tm, d), row), pl.BlockSpec((d, LANES), fixed), pl.BlockSpec((d, LANES), fixed)],
        out_specs=(pl.BlockSpec((tm, LANES), row), pl.BlockSpec((tm, LANES), row)),
        compiler_params=_params("parallel"),
        name="moe_router",
    )(xn, wh, wl)


def _row_dma_loop(n_rows, make_copy, wait):
    def body(g, c):
        for u in range(DMA_UNROLL):
            copy = make_copy(g * DMA_UNROLL + u)
            if wait:
                copy.wait()
            else:
                copy.start(priority=u % 2)
        return c

    lax.fori_loop(0, n_rows // DMA_UNROLL, body, 0)


def _gather_kernel(tok_ref, act_ref, x_hbm, o_ref, buf, sem, *, tm, n_blocks):
    i = pl.program_id(0)

    def rows_of(blk, slot):
        def make_copy(r):
            return pltpu.make_async_copy(x_hbm.at[pl.ds(tok_ref[blk * tm + r], 1)],
                                         buf.at[slot, pl.ds(r, 1)], sem.at[slot])
        return make_copy

    @pl.when(jnp.logical_and(i == 0, act_ref[0] == 1))
    def _():
        _row_dma_loop(tm, rows_of(0, 0), wait=False)

    nxt = jnp.minimum(i + 1, n_blocks - 1)

    @pl.when(jnp.logical_and(i + 1 < n_blocks, act_ref[nxt] == 1))
    def _():
        _row_dma_loop(tm, rows_of(nxt, nxt % 2), wait=False)

    @pl.when(act_ref[i] == 1)
    def _():
        _row_dma_loop(tm, rows_of(i, i % 2), wait=True)
        o_ref[...] = buf[i % 2].astype(o_ref.dtype)

    @pl.when(act_ref[i] == 0)
    def _():
        o_ref[...] = jnp.zeros(o_ref.shape, o_ref.dtype)


def moe_gather(xn, slot_tok, blk_act, *, tm):
    s = slot_tok.shape[0]
    d = xn.shape[1]
    assert tm % DMA_UNROLL == 0
    grid_spec = pltpu.PrefetchScalarGridSpec(
        num_scalar_prefetch=2,
        grid=(s // tm,),
        in_specs=[pl.BlockSpec(memory_space=pl.ANY)],
        out_specs=pl.BlockSpec((tm, d), lambda i, tok, act: (i, 0)),
        scratch_shapes=[pltpu.VMEM((2, tm, d), F32), pltpu.SemaphoreType.DMA((2,))],
    )
    return pl.pallas_call(
        functools.partial(_gather_kernel, tm=tm, n_blocks=s // tm),
        out_shape=jax.ShapeDtypeStruct((s, d), BF16),
        grid_spec=grid_spec,
        compiler_params=_params("arbitrary"),
        name="moe_gather",
    )(slot_tok, blk_act, xn)


def _combine_kernel(pos_ref, y_hbm, h_ref, gate_ref, gf_ref, o_first_ref, o_rest_ref, buf, sem,
                    *, tm, n_first, n_blocks):
    i = pl.program_id(0)

    def rows_of(blk, slot):
        def make_copy(r):
            return pltpu.make_async_copy(y_hbm.at[pl.ds(pos_ref[blk * tm * TOP_K + r], 1)],
                                         buf.at[slot, r % TOP_K, pl.ds(r // TOP_K, 1)], sem.at[slot])
        return make_copy

    @pl.when(i == 0)
    def _():
        _row_dma_loop(tm * TOP_K, rows_of(0, 0), wait=False)

    @pl.when(i + 1 < n_blocks)
    def _():
        _row_dma_loop(tm * TOP_K, rows_of(i + 1, (i + 1) % 2), wait=False)

    _row_dma_loop(tm * TOP_K, rows_of(i, i % 2), wait=True)
    gate = gate_ref[...]
    slot = i % 2
    y = buf[slot, 0] * gate[:, 0:1] + buf[slot, 1] * gate[:, 1:2]
    out = _rms(h_ref[...] + y, gf_ref[...])

    @pl.when(i < n_first)
    def _():
        o_first_ref[...] = out

    @pl.when(i >= n_first)
    def _():
        o_rest_ref[...] = out


def moe_combine(yb, pos, h, gate, g_final, *, tm, t_first):
    t, d = h.shape
    n_first = t_first // tm
    grid_spec = pltpu.PrefetchScalarGridSpec(
        num_scalar_prefetch=1,
        grid=(t // tm,),
        in_specs=[pl.BlockSpec(memory_space=pl.ANY),
                  pl.BlockSpec((tm, d), lambda i, pos: (i, 0)),
                  pl.BlockSpec((tm, LANES), lambda i, pos: (i, 0)),
                  pl.BlockSpec((1, d), lambda i, pos: (0, 0))],
        out_specs=(pl.BlockSpec((tm, d), lambda i, pos: (jnp.minimum(i, n_first - 1), 0)),
                   pl.BlockSpec((tm, d), lambda i, pos: (jnp.maximum(i - n_first, 0), 0))),
        scratch_shapes=[pltpu.VMEM((2, TOP_K, tm, d), F32), pltpu.SemaphoreType.DMA((2,))],
    )
    assert (tm * TOP_K) % DMA_UNROLL == 0 and DMA_UNROLL % TOP_K == 0
    return pl.pallas_call(
        functools.partial(_combine_kernel, tm=tm, n_first=n_first, n_blocks=t // tm),
        out_shape=(jax.ShapeDtypeStruct((t_first, d), F32), jax.ShapeDtypeStruct((t - t_first, d), F32)),
        grid_spec=grid_spec,
        compiler_params=_params("arbitrary"),
        name="moe_combine",
    )(pos, yb, h, gate, g_final.reshape(1, d))


def _ret_qk_kernel(x_ref, w_ref, cos_ref, sin_ref, o_ref, *, heads_per_step, dk):
    r = jnp.dot(x_ref[...], w_ref[...], preferred_element_type=F32)
    cos, sin = cos_ref[...], sin_ref[...]
    half = dk // 2
    for h in range(heads_per_step):
        e = r[:, h * dk:h * dk + half]
        o = r[:, h * dk + half:(h + 1) * dk]
        o_ref[:, h * dk:h * dk + half] = (e * cos - o * sin).astype(o_ref.dtype)
        o_ref[:, h * dk + half:(h + 1) * dk] = (o * cos + e * sin).astype(o_ref.dtype)


def ret_qk(xn, w_qk, cos_t, sin_t, *, dk):
    t, d = xn.shape
    n = w_qk.shape[1]
    tm = _tile(t, (640, 512, 256, 128, 64, 32, 16, 8))
    tn = _tile(n, (1024, 512, 256))
    return pl.pallas_call(
        functools.partial(_ret_qk_kernel, heads_per_step=tn // dk, dk=dk),
        out_shape=jax.ShapeDtypeStruct((t, n), BF16),
        grid=(n // tn, t // tm),
        in_specs=[pl.BlockSpec((tm, d), lambda j, i: (i, 0)), pl.BlockSpec((d, tn), lambda j, i: (0, j)),
                  pl.BlockSpec((tm, dk // 2), lambda j, i: (i, 0)),
                  pl.BlockSpec((tm, dk // 2), lambda j, i: (i, 0))],
        out_specs=pl.BlockSpec((tm, tn), lambda j, i: (i, j)),
        compiler_params=_params("parallel", "parallel"),
        name="ret_qk",
    )(xn, w_qk, cos_t, sin_t)


def _ret_kernel(q_ref, k_ref, v_ref, g_ref, s0_ref, dmat_ref, qdec_ref, kdec_ref, o_ref, s_out_ref, s_scr,
                *, nc, hps, dk, dv):
    c = pl.program_id(2)

    @pl.when(c == 0)
    def _():
        s_scr[...] = s0_ref[...]

    for h in range(hps):
        ks, vs = slice(h * dk, (h + 1) * dk), slice(h * dv, (h + 1) * dv)
        q, k, v = q_ref[:, ks], k_ref[:, ks], v_ref[:, vs]
        state = s_scr[h]
        qdec = qdec_ref[h][:, 0:1]
        kdec = kdec_ref[h][:, 0:1]
        scores = lax.dot_general(q, k, (((1,), (1,)), ((), ())), preferred_element_type=F32) * dmat_ref[h]
        intra = jnp.dot(scores.astype(BF16), v, preferred_element_type=F32)
        inter = jnp.dot(q, state.astype(BF16), preferred_element_type=F32) * qdec
        o = intra + inter
        kd = (k.astype(F32) * kdec).astype(BF16)
        chunk_decay = qdec[q.shape[0] - 1:, :]
        new_state = chunk_decay * state + lax.dot_general(kd, v, (((0,), (0,)), ((), ())),
                                                          preferred_element_type=F32)
        s_scr[h] = new_state
        g = g_ref[:, vs].astype(F32)
        o_ref[:, vs] = (g * jax.nn.sigmoid(g) * _rms(o)).astype(o_ref.dtype)

    @pl.when(c == nc - 1)
    def _():
        s_out_ref[...] = s_scr[...]


def retention(qk, v, g, s0, log_g, *, batch, heads, length, chunk, row0):
    dk, dv = s0.shape[2], s0.shape[3]
    nc = length // chunk
    blk0 = row0 // chunk
    idx = jnp.arange(chunk, dtype=F32)
    diff = idx[:, None] - idx[None, :]
    dmat = jnp.exp(jnp.where(diff[None] >= 0, diff[None] * log_g[:, None, None], -jnp.inf))
    qdec = jnp.broadcast_to(jnp.exp((idx[None, :] + 1.0) * log_g[:, None])[:, :, None], (heads, chunk, LANES))
    kdec = jnp.broadcast_to(jnp.exp((chunk - 1.0 - idx)[None, :] * log_g[:, None])[:, :, None], (heads, chunk, LANES))
    row = lambda b, h, c: blk0 + b * nc + c
    hps = _tile(heads, (4, 2, 1))
    hg = heads // hps
    return pl.pallas_call(
        functools.partial(_ret_kernel, nc=nc, hps=hps, dk=dk, dv=dv),
        out_shape=(jax.ShapeDtypeStruct((batch * length, heads * dv), BF16),
                   jax.ShapeDtypeStruct((batch, heads, dk, dv), F32)),
        grid=(batch, hg, nc),
        in_specs=[pl.BlockSpec((chunk, hps * dk), lambda b, h, c: (row(b, h, c), h)),
                  pl.BlockSpec((chunk, hps * dk), lambda b, h, c: (row(b, h, c), hg + h)),
                  pl.BlockSpec((chunk, hps * dv), lambda b, h, c: (row(b, h, c), h)),
                  pl.BlockSpec((chunk, hps * dv), lambda b, h, c: (row(b, h, c), h)),
                  pl.BlockSpec((None, hps, dk, dv), lambda b, h, c: (b, h, 0, 0)),
                  pl.BlockSpec((hps, chunk, chunk), lambda b, h, c: (h, 0, 0)),
                  pl.BlockSpec((hps, chunk, LANES), lambda b, h, c: (h, 0, 0)),
                  pl.BlockSpec((hps, chunk, LANES), lambda b, h, c: (h, 0, 0))],
        out_specs=(pl.BlockSpec((chunk, hps * dv), lambda b, h, c: (b * nc + c, h)),
                   pl.BlockSpec((None, hps, dk, dv), lambda b, h, c: (b, h, 0, 0))),
        scratch_shapes=[pltpu.VMEM((hps, dk, dv), F32)],
        compiler_params=_params("parallel", "parallel", "arbitrary"),
        name="retention",
    )(qk, qk, v, g, s0, dmat, qdec, kdec)


def _rope_half_tables(pos):
    half = MLA_ROPE // 2
    inv = 1.0 / (MLA_ROPE_BASE ** (jnp.arange(half, dtype=F32) / half))
    ang = pos.astype(F32)[:, None] * inv[None, :]
    c, s = jnp.cos(ang), jnp.sin(ang)
    z = jnp.zeros((pos.shape[0], LANES - MLA_ROPE), F32)
    return jnp.concatenate([c, c, z], axis=-1), jnp.concatenate([s, s, z], axis=-1)


def _rope_pair_tables(pos, half):
    inv = 1.0 / (RET_ROPE_BASE ** jnp.linspace(0.0, 1.0, half, dtype=F32))
    ang = pos.astype(F32)[:, None] * inv[None, :]
    return jnp.cos(ang), jnp.sin(ang)


def _rot_half_cols(w):
    half = w.shape[-1] // 2
    return jnp.concatenate([-w[..., half:], w[..., :half]], axis=-1)


def _pad_lanes(w):
    return jnp.pad(w, [(0, 0)] * (w.ndim - 1) + [(0, LANES - w.shape[-1])])


def _even_odd_cols(w, heads, dk):
    d = w.shape[0]
    return w.reshape(d, heads, dk // 2, 2).transpose(0, 1, 3, 2).reshape(d, heads * dk)


def _even_odd_rows(s, inverse=False):
    b, h, dk, dv = s.shape
    if inverse:
        return s.reshape(b, h, 2, dk // 2, dv).transpose(0, 1, 3, 2, 4).reshape(b, h, dk, dv)
    return s.reshape(b, h, dk // 2, 2, dv).transpose(0, 1, 3, 2, 4).reshape(b, h, dk, dv)


def _last_active(act):
    return lax.cummax(jnp.where(act > 0, jnp.arange(act.shape[0], dtype=jnp.int32), 0), axis=0)


def _route(idx, n_tok):
    n_assign = n_tok * TOP_K
    flat_e = idx.reshape(-1)
    onehot = (flat_e[:, None] == jnp.arange(N_EXPERTS, dtype=jnp.int32)[None, :]).astype(jnp.int32)
    csum = jnp.cumsum(onehot, axis=0)
    rank = jnp.sum((csum - onehot) * onehot, axis=1)
    counts = csum[-1]
    padded = (counts + MOE_SUPER - 1) // MOE_SUPER * MOE_SUPER
    pad_end = jnp.cumsum(padded)
    pad_start = pad_end - padded
    dest = (jnp.sum(pad_start[None, :] * onehot, axis=1) + rank).astype(jnp.int32)
    n_super = (n_assign + N_EXPERTS * (MOE_SUPER - 1)) // MOE_SUPER
    per_super = MOE_SUPER // MOE_BLOCK
    flat_tok = jnp.arange(n_assign, dtype=jnp.int32) // TOP_K
    slot_tok = jnp.zeros((n_super * MOE_SUPER,), jnp.int32).at[dest].set(flat_tok)
    sup_start = jnp.arange(n_super, dtype=jnp.int32) * MOE_SUPER
    sup_e = jnp.minimum(jnp.sum(sup_start[:, None] >= pad_end[None, :], axis=1), N_EXPERTS - 1).astype(jnp.int32)
    used = jnp.clip(counts[sup_e] - (sup_start - pad_start[sup_e]), 0, MOE_SUPER)
    sup_nact = ((used + MOE_BLOCK - 1) // MOE_BLOCK).astype(jnp.int32)
    blk_e = jnp.repeat(sup_e, per_super)
    blk_act = (jnp.arange(per_super, dtype=jnp.int32)[None, :] < sup_nact[:, None]).astype(jnp.int32).reshape(-1)
    return slot_tok, dest, (sup_e, sup_nact, _last_active(sup_nact)), (blk_e, blk_act, _last_active(blk_act))


def kernel(x_prompt, x_sample, mem_prompt, cache_mla_ckv, cache_mla_krope, state_ret, cache_mem_k, cache_mem_v, norm_mix, norm_xattn, norm_ffn, norm_final, mla_w_dq, mla_g_q, mla_w_uq, mla_w_dkv, mla_g_kv, mla_w_uk, mla_w_uv, mla_w_o, ret_w_q, ret_w_k, ret_w_v, ret_w_g, ret_w_o, xa_w_q, xa_w_k, xa_w_v, xa_w_o, ffn_w1, ffn_w3, ffn_w2, moe_w_router, moe_w1, moe_w3, moe_w2):
    bp, lp, d = x_prompt.shape
    bs, ls, _ = x_sample.shape
    past = cache_mla_ckv.shape[2]
    n_mem = mem_prompt.shape[1]
    tp, ts = bp * lp, bs * ls
    t = tp + ts
    heads = mla_w_uk.shape[2]
    q_rank, kv_rank = mla_w_dq.shape[2], mla_w_uk.shape[1]
    dk = d // RET_HEADS
    dv = 2 * dk
    xa_dim = XA_HEADS * XA_HEAD_DIM

    pos = jnp.concatenate([jnp.tile(jnp.arange(lp, dtype=jnp.int32), bp),
                           jnp.tile(past + jnp.arange(ls, dtype=jnp.int32), bs)])
    x_p, x_s = x_prompt.reshape(tp, d), x_sample.reshape(ts, d)
    tm_x = _tile(lp, (128, 64, 32))

    def cross_attention(h_p, h_s, layer, k_p, v_p, xn_dtype):
        wq, wo = xa_w_q[layer].astype(BF16), xa_w_o[layer].astype(BF16)
        k_all = jnp.concatenate([k_p, cache_mem_k[layer].reshape(bs * n_mem, xa_dim)], axis=0).astype(BF16)
        v_all = jnp.concatenate([v_p, cache_mem_v[layer].reshape(bs * n_mem, xa_dim)], axis=0).astype(BF16)
        return xattn_block(h_p, h_s, norm_xattn[layer], wq, k_all, v_all, wo, norm_ffn[layer], n_long=bp,
                           len_long=lp, len_short=ls, n_mem=n_mem, tm=tm_x, xn_dtype=xn_dtype)

    def memory_kv(layer):
        w = jnp.concatenate([xa_w_k[layer], xa_w_v[layer]], axis=1).astype(BF16)
        kv = matmul(mem_prompt.reshape(bp * n_mem, d).astype(BF16), w, out_dtype=F32, name="mem_kv")
        return kv[:, :xa_dim], kv[:, xa_dim:]

    xn = rmsnorm((x_p, x_s), norm_mix[0], BF16)
    cos_m, sin_m = _rope_half_tables(pos)
    w_dkv = mla_w_dkv[0]
    w_kr = w_dkv[:, kv_rank:]
    w_pre = jnp.concatenate([mla_w_dq[0], w_dkv[:, :kv_rank], _pad_lanes(w_kr), _pad_lanes(_rot_half_cols(w_kr))],
                            axis=1).astype(BF16)
    cq, ckv, ckv_b, kr, kr_b = mla_pre(xn, w_pre, mla_g_q[0], mla_g_kv[0], cos_m, sin_m)

    w_uq = mla_w_uq[0].reshape(q_rank, heads, MLA_NOPE + MLA_ROPE)
    wn = w_uq[:, :, :MLA_NOPE].reshape(q_rank, heads * LANES).astype(BF16)
    w_rope = w_uq[:, :, MLA_NOPE:]
    wr = jnp.concatenate([w_rope, _rot_half_cols(w_rope)], axis=-1).reshape(q_rank, heads * LANES).astype(BF16)
    cs_m = jnp.concatenate([cos_m[:, :MLA_ROPE], sin_m[:, :MLA_ROPE]], axis=-1)
    q = mla_q(cq, wn, wr, cs_m, scale=float((MLA_NOPE + MLA_ROPE) ** -0.5 * 1.4426950408889634))

    w_ukv = jnp.concatenate([mla_w_uk[0].reshape(kv_rank, heads * MLA_NOPE),
                             mla_w_uv[0].reshape(kv_rank, heads * MLA_V)], axis=1).astype(BF16)
    kv = matmul(ckv_b, w_ukv, out_dtype=BF16, name="mla_kv")
    t_att = _tile(lp, (1024, 512, 256, 128, 64))
    o_p = flash_attention(q, kv, kr_b, batch=bp, heads=heads, lq=lp, lk=lp, lk_valid=lp, q_off=0,
                          tq=t_att, tk=t_att)

    lk_s = past + ls
    lk_pad = -(-lk_s // LANES) * LANES
    ckv_all = jnp.concatenate([cache_mla_ckv[0].astype(BF16), ckv_b[tp:].reshape(bs, ls, kv_rank),
                               jnp.zeros((bs, lk_pad - lk_s, kv_rank), BF16)], axis=1)
    kr_all = jnp.concatenate([_pad_lanes(cache_mla_krope[0]).astype(BF16), kr_b[tp:].reshape(bs, ls, LANES),
                              jnp.zeros((bs, lk_pad - lk_s, LANES), BF16)], axis=1)
    kv_s = matmul(ckv_all.reshape(bs * lk_pad, kv_rank), w_ukv, out_dtype=BF16, name="mla_kv_sample")
    o_s = flash_attention(q[tp:], kv_s, kr_all.reshape(bs * lk_pad, LANES), batch=bs, heads=heads, lq=ls,
                          lk=lk_pad, lk_valid=lk_s, q_off=past, tq=ls, tk=lk_pad)
    w_o = mla_w_o[0].astype(BF16)
    h1_p = matmul(o_p, w_o, out_dtype=F32, res=x_p, name="mla_out")
    h1_s = matmul(o_s, w_o, out_dtype=F32, res=x_s, name="mla_out_sample")

    mk0, mv0 = memory_kv(0)
    h2, xn_f = cross_attention(h1_p, h1_s, 0, mk0, mv0, BF16)

    d_ff = ffn_w1.shape[2]
    hf = dense_gateup(xn_f, ffn_w1[0].astype(BF16), ffn_w3[0].astype(BF16),
                      tm=_tile(t, (1280, 640, 256, 128, 64, 32)), tf=_tile(d_ff, (512, 256, 128)))
    h3 = matmul(hf, ffn_w2[0].astype(BF16), out_dtype=F32, res=h2, name="ffn_down")

    xn = rmsnorm(h3, norm_mix[1], BF16)
    cos_r, sin_r = _rope_pair_tables(pos, dk // 2)
    w_qk = jnp.concatenate([_even_odd_cols(ret_w_q[0].astype(BF16), RET_HEADS, dk),
                            _even_odd_cols((ret_w_k[0] * (dk ** -0.5)).astype(BF16), RET_HEADS, dk)], axis=1)
    qk = ret_qk(xn, w_qk, cos_r, sin_r, dk=dk)
    rv = matmul(xn, ret_w_v[0].astype(BF16), out_dtype=BF16, name="ret_v")
    rg = matmul(xn, ret_w_g[0].astype(BF16), out_dtype=BF16, name="ret_g")
    log_g = jnp.log1p(-jnp.exp2(-5.0 - jnp.arange(RET_HEADS, dtype=F32)))
    ch_p = _tile(lp, (256, 128, 64))
    og_p, st_p = retention(qk, rv, rg, jnp.zeros((bp, RET_HEADS, dk, dv), F32), log_g, batch=bp, heads=RET_HEADS,
                           length=lp, chunk=ch_p, row0=0)
    og_s, st_s = retention(qk, rv, rg, _even_odd_rows(state_ret[0]), log_g, batch=bs, heads=RET_HEADS,
                           length=ls, chunk=ls, row0=tp)
    w_o = ret_w_o[0].astype(BF16)
    h4_p = matmul(og_p, w_o, out_dtype=F32, res=h3, name="ret_out")
    h4_s = matmul(og_s, w_o, out_dtype=F32, res=h3, res_row0=tp, name="ret_out_sample")

    mk1, mv1 = memory_kv(1)
    h5, xn_f = cross_attention(h4_p, h4_s, 1, mk1, mv1, F32)

    idx_l, gate_l = moe_router(xn_f, moe_w_router[0])
    slot_tok, dest, sup, blk = _route(idx_l[:, :TOP_K], t)
    xs = moe_gather(xn_f, slot_tok, blk[1], tm=MOE_BLOCK)
    hs = moe_gateup(xs, moe_w1[0], moe_w3[0], *sup, tm=MOE_SUPER, tf=_tile(d_ff, (512, 256, 128)))
    yb = moe_down(hs, moe_w2[0].astype(BF16), *blk, tm=MOE_BLOCK, tn=_tile(d, (512, 256, 128)))
    y_p, y_s = moe_combine(yb, dest, h5, gate_l, norm_final, tm=_tile(ts, (128, 64, 32, 16, 8)), t_first=tp)

    def split(a, width):
        return a[:tp].reshape(1, bp, lp, width), a[tp:].reshape(1, bs, ls, width)

    ckv_p, ckv_s = split(ckv, kv_rank)
    kr_p, kr_s = split(kr, MLA_ROPE)
    mem_shape = (bp, n_mem, XA_HEADS, XA_HEAD_DIM)
    return (y_p.reshape(bp, lp, d), y_s.reshape(bs, ls, d),
            ckv_p, kr_p, _even_odd_rows(st_p, inverse=True)[None],
            jnp.stack([mk0.reshape(mem_shape), mk1.reshape(mem_shape)]),
            jnp.stack([mv0.reshape(mem_shape), mv1.reshape(mem_shape)]),
            ckv_s, kr_s, _even_odd_rows(st_s, inverse=True)[None])
```

```python
import functools

import jax
import jax.numpy as jnp
from jax import lax
from jax.experimental import pallas as pl
from jax.experimental.pallas import tpu as pltpu

F32 = jnp.float32
BF16 = jnp.bfloat16

RMS_EPS = 1e-6
CHUNK = 64
MLA_NOPE = 128
MLA_ROPE = 64
MLA_V = 128
MLA_ROPE_BASE = 10000.0
RET_HEADS = 16
RET_ROPE_BASE = 10000.0
XA_HEADS = 4
XA_HEAD_DIM = 128
N_EXPERTS = 8
TOP_K = 2
MOE_BLOCK = 256
MOE_SUPER = 1024
DMA_UNROLL = 8
LANES = 128
NEG_BIG = -1e30
VMEM_LIMIT_BYTES = 56 * 1024 * 1024


def _params(*sem):
    return pltpu.CompilerParams(dimension_semantics=sem, vmem_limit_bytes=VMEM_LIMIT_BYTES)


def _tile(n, candidates):
    for c in candidates:
        if c <= n and n % c == 0:
            return c
    return n


def _rms(x, g=None):
    y = x * lax.rsqrt(jnp.mean(x * x, axis=-1, keepdims=True) + RMS_EPS)
    return y if g is None else y * g


def _two_part_specs(tm, d, n_first):
    return [pl.BlockSpec((tm, d), lambda i: (jnp.minimum(i, n_first - 1), 0)),
            pl.BlockSpec((tm, d), lambda i: (jnp.maximum(i - n_first, 0), 0))]


def _two_part_tile(first_ref, rest_ref, n_first):
    return jnp.where(pl.program_id(0) < n_first, first_ref[...], rest_ref[...])


def _rmsnorm_kernel(x_ref, g_ref, o_ref):
    o_ref[...] = _rms(x_ref[...], g_ref[...]).astype(o_ref.dtype)


def _rmsnorm2_kernel(xa_ref, xb_ref, g_ref, o_ref, *, n_first):
    o_ref[...] = _rms(_two_part_tile(xa_ref, xb_ref, n_first), g_ref[...]).astype(o_ref.dtype)


def rmsnorm(x, g, out_dtype):
    first, rest = x if isinstance(x, tuple) else (x, None)
    d = first.shape[1]
    t = first.shape[0] + (0 if rest is None else rest.shape[0])
    tm = _tile(t if rest is None else rest.shape[0], (256, 128, 64, 32, 16, 8))
    row = pl.BlockSpec((tm, d), lambda i: (i, 0))
    gain = pl.BlockSpec((1, d), lambda i: (0, 0))
    if rest is None:
        body, in_specs, args = _rmsnorm_kernel, [row, gain], (first,)
    else:
        n_first = first.shape[0] // tm
        body = functools.partial(_rmsnorm2_kernel, n_first=n_first)
        in_specs, args = _two_part_specs(tm, d, n_first) + [gain], (first, rest)
    return pl.pallas_call(
        body,
        out_shape=jax.ShapeDtypeStruct((t, d), out_dtype),
        grid=(t // tm,),
        in_specs=in_specs,
        out_specs=row,
        compiler_params=_params("parallel"),
        name="rmsnorm",
    )(*args, g.reshape(1, d))


def _mm_kernel(*refs, nk, has_res):
    a_ref, w_ref = refs[0], refs[1]
    res_ref = refs[2] if has_res else None
    o_ref = refs[2 + has_res]
    part = jnp.dot(a_ref[...], w_ref[...], preferred_element_type=F32)
    if nk == 1:
        if has_res:
            part = part + res_ref[...]
        o_ref[...] = part.astype(o_ref.dtype)
        return
    acc_ref = refs[3 + has_res]
    k = pl.program_id(2)

    @pl.when(k == 0)
    def _():
        acc_ref[...] = part

    @pl.when(k > 0)
    def _():
        acc_ref[...] += part

    @pl.when(k == nk - 1)
    def _():
        out = acc_ref[...]
        if has_res:
            out = out + res_ref[...]
        o_ref[...] = out.astype(o_ref.dtype)


def matmul(a, w, *, out_dtype, res=None, res_row0=0, tm=None, tn=None, tk=None, name="matmul"):
    m, kdim = a.shape
    n = w.shape[1]
    tk = tk or (kdim if kdim <= 4096 else _tile(kdim, (2048, 1024, 512)))
    row_tiles = (1280, 1024) if tk <= 2048 else ()
    tm = tm or _tile(m, row_tiles + (640, 512, 256, 128, 64, 32, 16, 8))
    tn = tn or _tile(n, (1024, 512, 256, 128))
    nk = kdim // tk
    has_res = res is not None
    in_specs = [pl.BlockSpec((tm, tk), lambda j, i, k: (i, k)),
                pl.BlockSpec((tk, tn), lambda j, i, k: (k, j))]
    args = [a, w]
    if has_res:
        assert res_row0 % tm == 0
        in_specs.append(pl.BlockSpec((tm, tn), lambda j, i, k: (i + res_row0 // tm, j)))
        args.append(res)
    scratch = [pltpu.VMEM((tm, tn), F32)] if nk > 1 else []
    return pl.pallas_call(
        functools.partial(_mm_kernel, nk=nk, has_res=has_res),
        out_shape=jax.ShapeDtypeStruct((m, n), out_dtype),
        grid=(n // tn, m // tm, nk),
        in_specs=in_specs,
        out_specs=pl.BlockSpec((tm, tn), lambda j, i, k: (i, j)),
        scratch_shapes=scratch,
        compiler_params=_params("parallel", "parallel", "arbitrary"),
        name=name,
    )(*args)


def _mla_pre_kernel(x_ref, w_ref, gq_ref, gkv_ref, cos_ref, sin_ref,
                    cq_ref, ckv_ref, ckvb_ref, kr_ref, krb_ref, *, q_rank, kv_rank):
    r = jnp.dot(x_ref[...], w_ref[...], preferred_element_type=F32)
    cq_ref[...] = _rms(r[:, :q_rank], gq_ref[...]).astype(cq_ref.dtype)
    ckv = _rms(r[:, q_rank:q_rank + kv_rank], gkv_ref[...])
    ckv_ref[...] = ckv
    ckvb_ref[...] = ckv.astype(ckvb_ref.dtype)
    o = q_rank + kv_rank
    kr = r[:, o:o + LANES] * cos_ref[...] + r[:, o + LANES:o + 2 * LANES] * sin_ref[...]
    kr_ref[...] = kr[:, :MLA_ROPE]
    krb_ref[...] = kr.astype(krb_ref.dtype)


def mla_pre(xn, w_pre, g_q, g_kv, cos_t, sin_t):
    t, d = xn.shape
    q_rank, kv_rank = g_q.shape[0], g_kv.shape[0]
    n = w_pre.shape[1]
    tm = _tile(t, (256, 128, 64, 32, 16, 8))
    row = lambda i: (i, 0)
    fixed = lambda i: (0, 0)
    return pl.pallas_call(
        functools.partial(_mla_pre_kernel, q_rank=q_rank, kv_rank=kv_rank),
        out_shape=(jax.ShapeDtypeStruct((t, q_rank), BF16),
                   jax.ShapeDtypeStruct((t, kv_rank), F32),
                   jax.ShapeDtypeStruct((t, kv_rank), BF16),
                   jax.ShapeDtypeStruct((t, MLA_ROPE), F32),
                   jax.ShapeDtypeStruct((t, LANES), BF16)),
        grid=(t // tm,),
        in_specs=[pl.BlockSpec((tm, d), row), pl.BlockSpec((d, n), fixed),
                  pl.BlockSpec((1, q_rank), fixed), pl.BlockSpec((1, kv_rank), fixed),
                  pl.BlockSpec((tm, LANES), row), pl.BlockSpec((tm, LANES), row)],
        out_specs=(pl.BlockSpec((tm, q_rank), row), pl.BlockSpec((tm, kv_rank), row),
                   pl.BlockSpec((tm, kv_rank), row), pl.BlockSpec((tm, MLA_ROPE), row),
                   pl.BlockSpec((tm, LANES), row)),
        compiler_params=_params("parallel"),
        name="mla_pre",
    )(xn, w_pre, g_q.reshape(1, -1), g_kv.reshape(1, -1), cos_t, sin_t)


def _mla_q_kernel(c_ref, wn_ref, wr_ref, cs_ref, q_ref, *, heads_per_step, scale):
    c = c_ref[...]
    qn = jnp.dot(c, wn_ref[...], preferred_element_type=F32) * scale
    qr = jnp.dot(c, wr_ref[...], preferred_element_type=F32)
    cs = cs_ref[...] * scale
    for h in range(heads_per_step):
        lo, hi = h * LANES, (h + 1) * LANES
        q_ref[:, 2 * lo:2 * lo + LANES] = qn[:, lo:hi].astype(q_ref.dtype)
        terms = qr[:, lo:hi] * cs
        q_ref[:, 2 * lo + LANES:2 * hi] = (terms + pltpu.roll(terms, LANES // 2, 1)).astype(q_ref.dtype)


def mla_q(cq, wn, wr, cs_t, *, scale):
    t, q_rank = cq.shape
    n = wn.shape[1]
    tm = _tile(t, (1280, 1024, 640, 512, 256, 128, 64, 32, 16, 8))
    tn = _tile(n, (512, 256, 128))
    w_spec = pl.BlockSpec((q_rank, tn), lambda j, i: (0, j))
    return pl.pallas_call(
        functools.partial(_mla_q_kernel, heads_per_step=tn // LANES, scale=scale),
        out_shape=jax.ShapeDtypeStruct((t, 2 * n), BF16),
        grid=(n // tn, t // tm),
        in_specs=[pl.BlockSpec((tm, q_rank), lambda j, i: (i, 0)), w_spec, w_spec,
                  pl.BlockSpec((tm, LANES), lambda j, i: (i, 0))],
        out_specs=pl.BlockSpec((tm, 2 * tn), lambda j, i: (i, j)),
        compiler_params=_params("parallel", "parallel"),
        name="mla_q",
    )(cq, wn, wr, cs_t)


def _flash_last_k(qi, *, tq, tk, nk, q_off):
    last_pos = ((q_off + (qi + 1) * tq - 1) // CHUNK) * CHUNK + CHUNK - 1
    return min(nk - 1, last_pos // tk)


def _flash_kernel(qi_ref, ki_ref, q_ref, kn_ref, kr_ref, v_ref, o_ref, m_scr, l_scr, acc_scr,
                  *, tq, tk, rb, hp, nk, q_off, lk_valid, aligned):
    step = pl.program_id(2)
    qi, ki = qi_ref[step], ki_ref[step]
    first_q = q_off + qi * tq
    last_seen_by_first_row = (first_q // CHUNK) * CHUNK + CHUNK - 1
    tile_last_key = ki * tk + tk - 1
    needs_mask = jnp.logical_or(tile_last_key > last_seen_by_first_row, tile_last_key >= lk_valid)
    last_ki = jnp.minimum(nk - 1, (((first_q + tq - 1) // CHUNK) * CHUNK + CHUNK - 1) // tk)
    shift = CHUNK.bit_length() - 1

    @pl.when(ki == 0)
    def _():
        m_scr[...] = jnp.full(m_scr.shape, NEG_BIG, F32)
        l_scr[...] = jnp.zeros(l_scr.shape, F32)
        acc_scr[...] = jnp.zeros(acc_scr.shape, F32)

    def attend(masked):
        block = rb if (masked and aligned) else tq
        for h in range(hp):
            hl = slice(h * LANES, (h + 1) * LANES)
            for r in range(tq // block):
                rows = slice(r * block, (r + 1) * block)
                nc = min(tk, (r + 1) * block) if (masked and aligned) else tk
                k = jnp.concatenate([kn_ref[:nc, hl], kr_ref[:nc, :]], axis=-1)
                s = lax.dot_general(q_ref[rows, 2 * h * LANES:2 * (h + 1) * LANES], k, (((1,), (1,)), ((), ())),
                                    preferred_element_type=F32)
                if masked:
                    qpos = first_q + r * block + lax.broadcasted_iota(jnp.int32, (block, nc), 0)
                    kpos = ki * tk + lax.broadcasted_iota(jnp.int32, (block, nc), 1)
                    s = jnp.where(lax.shift_right_arithmetic(kpos, shift) <= lax.shift_right_arithmetic(qpos, shift),
                                  s, NEG_BIG)
                    if lk_valid < nk * tk:
                        s = jnp.where(kpos < lk_valid, s, NEG_BIG)
                m_prev = m_scr[h, rows, :]
                m_new = jnp.maximum(m_prev, jnp.max(s, axis=-1, keepdims=True))
                alpha = jnp.exp2(m_prev - m_new)
                p = jnp.exp2(s - jnp.tile(m_new, (1, nc // LANES)))
                l_scr[h, rows, :] = alpha * l_scr[h, rows, :] + jnp.sum(p, axis=-1, keepdims=True)
                acc_scr[rows, hl] = alpha * acc_scr[rows, hl] + jnp.dot(p.astype(BF16), v_ref[:nc, hl],
                                                                       preferred_element_type=F32)
                m_scr[h, rows, :] = m_new

    pl.when(needs_mask)(functools.partial(attend, True))
    pl.when(jnp.logical_not(needs_mask))(functools.partial(attend, False))

    @pl.when(ki == last_ki)
    def _():
        for h in range(hp):
            hl = slice(h * LANES, (h + 1) * LANES)
            o_ref[:, hl] = (acc_scr[:, hl] / l_scr[h]).astype(o_ref.dtype)


def flash_attention(q, kv, kr, *, batch, heads, lq, lk, lk_valid, q_off, tq, tk):
    assert CHUNK & (CHUNK - 1) == 0 and MLA_V == LANES and MLA_NOPE == LANES
    nq, nk = lq // tq, lk // tk
    pairs = [(qi, ki) for qi in range(nq)
             for ki in range(_flash_last_k(qi, tq=tq, tk=tk, nk=nk, q_off=q_off) + 1)]
    qi_of = jnp.asarray([p[0] for p in pairs], jnp.int32)
    ki_of = jnp.asarray([p[1] for p in pairs], jnp.int32)
    hp = _tile(heads, (4, 2, 1))
    hg = heads // hp
    rb = _tile(tq, (256, 128, 64, 32, 16))
    aligned = q_off % tq == 0 and tq == tk and lk_valid == lk and rb % max(CHUNK, LANES) == 0
    grid_spec = pltpu.PrefetchScalarGridSpec(
        num_scalar_prefetch=2,
        grid=(batch, hg, len(pairs)),
        in_specs=[pl.BlockSpec((tq, 2 * hp * LANES), lambda b, h, s, qi, ki: (b * nq + qi[s], h)),
                  pl.BlockSpec((tk, hp * LANES), lambda b, h, s, qi, ki: (b * nk + ki[s], h)),
                  pl.BlockSpec((tk, LANES), lambda b, h, s, qi, ki: (b * nk + ki[s], 0)),
                  pl.BlockSpec((tk, hp * LANES), lambda b, h, s, qi, ki: (b * nk + ki[s], hg + h))],
        out_specs=pl.BlockSpec((tq, hp * LANES), lambda b, h, s, qi, ki: (b * nq + qi[s], h)),
        scratch_shapes=[pltpu.VMEM((hp, tq, LANES), F32), pltpu.VMEM((hp, tq, LANES), F32),
                        pltpu.VMEM((tq, hp * LANES), F32)],
    )
    return pl.pallas_call(
        functools.partial(_flash_kernel, tq=tq, tk=tk, rb=rb, hp=hp, nk=nk, q_off=q_off, lk_valid=lk_valid,
                          aligned=aligned),
        out_shape=jax.ShapeDtypeStruct((batch * lq, heads * MLA_V), BF16),
        grid_spec=grid_spec,
        compiler_params=_params("parallel", "parallel", "arbitrary"),
        name="mla_flash",
    )(qi_of, ki_of, q, kv, kr, kv)


def _xattn_attend(q, k, v):
    scale = float(XA_HEAD_DIM ** -0.5)
    outs = []
    for hd in range(XA_HEADS):
        sl = slice(hd * XA_HEAD_DIM, (hd + 1) * XA_HEAD_DIM)
        s = lax.dot_general(q[:, sl], k[:, sl], (((1,), (1,)), ((), ())), preferred_element_type=F32) * scale
        p = jnp.exp(s - jnp.max(s, axis=-1, keepdims=True))
        p = p / jnp.sum(p, axis=-1, keepdims=True)
        outs.append(jnp.dot(p.astype(BF16), v[:, sl], preferred_element_type=F32).astype(BF16))
    return jnp.concatenate(outs, axis=-1)


def _xattn_kernel(*refs, sub, n_long_tiles):
    ha_ref, hb_ref, gx_ref, wq_ref, wo_ref, gf_ref = refs[:6]
    refs = refs[1:]
    k_refs, v_refs = refs[5:5 + sub], refs[5 + sub:5 + 2 * sub]
    h_out, xn_out, o_scr = refs[5 + 2 * sub:]
    i = pl.program_id(0)
    x = _two_part_tile(ha_ref, hb_ref, n_long_tiles)
    xn = _rms(x, gx_ref[...]).astype(BF16)
    q = jnp.dot(xn, wq_ref[...], preferred_element_type=F32).astype(BF16)
    rows = q.shape[0] // sub

    @pl.when(i < n_long_tiles)
    def _():
        o_scr[...] = _xattn_attend(q, k_refs[0][...], v_refs[0][...])

    @pl.when(i >= n_long_tiles)
    def _():
        for j in range(sub):
            o_scr[j * rows:(j + 1) * rows, :] = _xattn_attend(q[j * rows:(j + 1) * rows], k_refs[j][...], v_refs[j][...])

    h_new = x + jnp.dot(o_scr[...], wo_ref[...], preferred_element_type=F32)
    h_out[...] = h_new
    xn_out[...] = _rms(h_new, gf_ref[...]).astype(xn_out.dtype)


def xattn_block(h_long, h_short, g_x, w_q, k, v, w_o, g_f, *, n_long, len_long, len_short, n_mem, tm, xn_dtype):
    d = h_long.shape[1]
    t = h_long.shape[0] + h_short.shape[0]
    hd = XA_HEADS * XA_HEAD_DIM
    sub = tm // len_short
    tiles_per_long = len_long // tm
    n_long_tiles = n_long * tiles_per_long
    row = lambda i: (i, 0)
    fixed = lambda i: (0, 0)

    def mem(j):
        def index(i):
            short = n_long + (i - n_long_tiles) * sub + j
            return (jnp.where(i < n_long_tiles, i // tiles_per_long, short), 0)
        return pl.BlockSpec((n_mem, hd), index)

    in_specs = (_two_part_specs(tm, d, n_long_tiles)
                + [pl.BlockSpec((1, d), fixed), pl.BlockSpec((d, hd), fixed),
                   pl.BlockSpec((hd, d), fixed), pl.BlockSpec((1, d), fixed)]
                + [mem(j) for j in range(sub)] * 2)
    return pl.pallas_call(
        functools.partial(_xattn_kernel, sub=sub, n_long_tiles=n_long_tiles),
        out_shape=(jax.ShapeDtypeStruct((t, d), F32), jax.ShapeDtypeStruct((t, d), xn_dtype)),
        grid=(t // tm,),
        in_specs=in_specs,
        out_specs=(pl.BlockSpec((tm, d), row), pl.BlockSpec((tm, d), row)),
        scratch_shapes=[pltpu.VMEM((tm, hd), BF16)],
        compiler_params=_params("parallel"),
        name="xattn",
    )(h_long, h_short, g_x.reshape(1, d), w_q, w_o, g_f.reshape(1, d), *([k] * sub), *([v] * sub))


def _swiglu(x, w1, w3):
    a = jnp.dot(x, w1, preferred_element_type=F32)
    b = jnp.dot(x, w3, preferred_element_type=F32)
    return (a * jax.nn.sigmoid(a) * b).astype(BF16)


def _dense_gateup_kernel(x_ref, w1_ref, w3_ref, o_ref):
    o_ref[...] = _swiglu(x_ref[...], w1_ref[...], w3_ref[...])


def dense_gateup(x, w1, w3, *, tm, tf):
    t, d = x.shape
    f = w1.shape[1]
    w_spec = pl.BlockSpec((d, tf), lambda j, i: (0, j))
    return pl.pallas_call(
        _dense_gateup_kernel,
        out_shape=jax.ShapeDtypeStruct((t, f), BF16),
        grid=(f // tf, t // tm),
        in_specs=[pl.BlockSpec((tm, d), lambda j, i: (i, 0)), w_spec, w_spec],
        out_specs=pl.BlockSpec((tm, tf), lambda j, i: (i, j)),
        compiler_params=_params("parallel", "parallel"),
        name="dense_gateup",
    )(x, w1, w3)


def _moe_gateup_kernel(se_ref, nact_ref, src_ref, x_ref, w1_ref, w3_ref, o_ref, *, sub, n_sub):
    nact = nact_ref[pl.program_id(0)]

    @pl.when(nact == n_sub)
    def _():
        o_ref[...] = _swiglu(x_ref[...], w1_ref[...].astype(BF16), w3_ref[...].astype(BF16))

    @pl.when(nact < n_sub)
    def _():
        for b in range(n_sub):
            rows = slice(b * sub, (b + 1) * sub)

            @pl.when(b < nact)
            def _():
                o_ref[rows, :] = _swiglu(x_ref[rows, :], w1_ref[...].astype(BF16), w3_ref[...].astype(BF16))

            @pl.when(b >= nact)
            def _():
                o_ref[rows, :] = jnp.zeros((sub, o_ref.shape[1]), o_ref.dtype)


def moe_gateup(xs, w1, w3, sup_e, sup_nact, sup_src, *, tm, tf):
    s, d = xs.shape
    f = w1.shape[2]
    w_spec = pl.BlockSpec((None, d, tf), lambda i, j, se, na, src: (se[i], 0, jnp.where(na[i] > 0, j, 0)))
    grid_spec = pltpu.PrefetchScalarGridSpec(
        num_scalar_prefetch=3,
        grid=(s // tm, f // tf),
        in_specs=[pl.BlockSpec((tm, d), lambda i, j, se, na, src: (src[i], 0), pipeline_mode=pl.Buffered(1)),
                  w_spec, w_spec],
        out_specs=pl.BlockSpec((tm, tf), lambda i, j, se, na, src: (i, j)),
    )
    return pl.pallas_call(
        functools.partial(_moe_gateup_kernel, sub=MOE_BLOCK, n_sub=tm // MOE_BLOCK),
        out_shape=jax.ShapeDtypeStruct((s, f), BF16),
        grid_spec=grid_spec,
        compiler_params=_params("parallel", "arbitrary"),
        name="moe_gateup",
    )(sup_e, sup_nact, sup_src, xs, w1, w3)


def _moe_down_kernel(be_ref, act_ref, src_ref, h_ref, w_ref, o_ref):
    i = pl.program_id(1)

    @pl.when(act_ref[i] == 1)
    def _():
        o_ref[...] = jnp.dot(h_ref[...], w_ref[...], preferred_element_type=F32)

    @pl.when(act_ref[i] == 0)
    def _():
        o_ref[...] = jnp.zeros(o_ref.shape, o_ref.dtype)


def moe_down(hs, w2, blk_e, blk_act, blk_src, *, tm, tn):
    s, f = hs.shape
    d = w2.shape[2]
    grid_spec = pltpu.PrefetchScalarGridSpec(
        num_scalar_prefetch=3,
        grid=(d // tn, s // tm),
        in_specs=[pl.BlockSpec((tm, f), lambda j, i, be, act, src: (src[i], 0)),
                  pl.BlockSpec((None, f, tn), lambda j, i, be, act, src: (be[i], 0, j),
                               pipeline_mode=pl.Buffered(1))],
        out_specs=pl.BlockSpec((tm, tn), lambda j, i, be, act, src: (i, j)),
    )
    return pl.pallas_call(
        _moe_down_kernel,
        out_shape=jax.ShapeDtypeStruct((s, d), F32),
        grid_spec=grid_spec,
        compiler_params=_params("parallel", "arbitrary"),
        name="moe_down",
    )(blk_e, blk_act, blk_src, hs, w2)


def _router_kernel(x_ref, wh_ref, wl_ref, idx_ref, gate_ref):
    x = x_ref[...]
    xh = x.astype(BF16)
    xl = (x - xh.astype(F32)).astype(BF16)
    wh, wl = wh_ref[...], wl_ref[...]
    logits = (jnp.dot(xh, wh, preferred_element_type=F32) + jnp.dot(xl, wh, preferred_element_type=F32)
              + jnp.dot(xh, wl, preferred_element_type=F32))
    lane = lax.broadcasted_iota(jnp.int32, logits.shape, 1)
    logits = jnp.where(lane < N_EXPERTS, logits, -jnp.inf)
    m1 = jnp.max(logits, axis=-1, keepdims=True)
    i1 = jnp.min(jnp.where(logits == m1, lane, LANES), axis=-1, keepdims=True)
    rest = jnp.where(lane == i1, -jnp.inf, logits)
    m2 = jnp.max(rest, axis=-1, keepdims=True)
    i2 = jnp.min(jnp.where(rest == m2, lane, LANES), axis=-1, keepdims=True)
    e = jnp.exp(m2 - m1)
    g1 = 1.0 / (1.0 + e)
    g2 = e / (1.0 + e)
    idx_ref[...] = jnp.where(lane == 0, i1, jnp.where(lane == 1, i2, 0))
    gate_ref[...] = jnp.where(lane == 0, g1, jnp.where(lane == 1, g2, 0.0))


def moe_router(xn, w_router):
    t, d = xn.shape
    w = jnp.zeros((d, LANES), F32).at[:, :N_EXPERTS].set(w_router)
    wh = w.astype(BF16)
    wl = (w - wh.astype(F32)).astype(BF16)
    tm = _tile(t, (256, 128, 64, 32, 16, 8))
    row = lambda i: (i, 0)
    fixed = lambda i: (0, 0)
    return pl.pallas_call(
        _router_kernel,
        out_shape=(jax.ShapeDtypeStruct((t, LANES), jnp.int32), jax.ShapeDtypeStruct((t, LANES), F32)),
        grid=(t // tm,),
        in_specs=[pl.BlockSpec((tm, d), row), pl.BlockSpec((d, LANES), fixed), pl.BlockSpec((d, LANES), fixed)],
        out_specs=(pl.BlockSpec((tm, LANES), row), pl.BlockSpec((tm, LANES), row)),
        compiler_params=_params("parallel"),
        name="moe_router",
    )(xn, wh, wl)


def _row_dma_loop(n_rows, make_copy, wait):
    def body(g, c):
        for u in range(DMA_UNROLL):
            copy = make_copy(g * DMA_UNROLL + u)
            if wait:
                copy.wait()
            else:
                copy.start(priority=u % 2)
        return c

    lax.fori_loop(0, n_rows // DMA_UNROLL, body, 0)


def _gather_kernel(tok_ref, act_ref, x_hbm, o_ref, buf, sem, *, tm, n_blocks):
    i = pl.program_id(0)

    def rows_of(blk, slot):
        def make_copy(r):
            return pltpu.make_async_copy(x_hbm.at[pl.ds(tok_ref[blk * tm + r], 1)],
                                         buf.at[slot, pl.ds(r, 1)], sem.at[slot])
        return make_copy

    @pl.when(jnp.logical_and(i == 0, act_ref[0] == 1))
    def _():
        _row_dma_loop(tm, rows_of(0, 0), wait=False)

    nxt = jnp.minimum(i + 1, n_blocks - 1)

    @pl.when(jnp.logical_and(i + 1 < n_blocks, act_ref[nxt] == 1))
    def _():
        _row_dma_loop(tm, rows_of(nxt, nxt % 2), wait=False)

    @pl.when(act_ref[i] == 1)
    def _():
        _row_dma_loop(tm, rows_of(i, i % 2), wait=True)
        o_ref[...] = buf[i % 2].astype(o_ref.dtype)

    @pl.when(act_ref[i] == 0)
    def _():
        o_ref[...] = jnp.zeros(o_ref.shape, o_ref.dtype)


def moe_gather(xn, slot_tok, blk_act, *, tm):
    s = slot_tok.shape[0]
    d = xn.shape[1]
    assert tm % DMA_UNROLL == 0
    grid_spec = pltpu.PrefetchScalarGridSpec(
        num_scalar_prefetch=2,
        grid=(s // tm,),
        in_specs=[pl.BlockSpec(memory_space=pl.ANY)],
        out_specs=pl.BlockSpec((tm, d), lambda i, tok, act: (i, 0)),
        scratch_shapes=[pltpu.VMEM((2, tm, d), F32), pltpu.SemaphoreType.DMA((2,))],
    )
    return pl.pallas_call(
        functools.partial(_gather_kernel, tm=tm, n_blocks=s // tm),
        out_shape=jax.ShapeDtypeStruct((s, d), BF16),
        grid_spec=grid_spec,
        compiler_params=_params("arbitrary"),
        name="moe_gather",
    )(slot_tok, blk_act, xn)


def _combine_kernel(pos_ref, y_hbm, h_ref, gate_ref, gf_ref, o_first_ref, o_rest_ref, buf, sem,
                    *, tm, n_first, n_blocks):
    i = pl.program_id(0)

    def rows_of(blk, slot):
        def make_copy(r):
            return pltpu.make_async_copy(y_hbm.at[pl.ds(pos_ref[blk * tm * TOP_K + r], 1)],
                                         buf.at[slot, r % TOP_K, pl.ds(r // TOP_K, 1)], sem.at[slot])
        return make_copy

    @pl.when(i == 0)
    def _():
        _row_dma_loop(tm * TOP_K, rows_of(0, 0), wait=False)

    @pl.when(i + 1 < n_blocks)
    def _():
        _row_dma_loop(tm * TOP_K, rows_of(i + 1, (i + 1) % 2), wait=False)

    _row_dma_loop(tm * TOP_K, rows_of(i, i % 2), wait=True)
    gate = gate_ref[...]
    slot = i % 2
    y = buf[slot, 0] * gate[:, 0:1] + buf[slot, 1] * gate[:, 1:2]
    out = _rms(h_ref[...] + y, gf_ref[...])

    @pl.when(i < n_first)
    def _():
        o_first_ref[...] = out

    @pl.when(i >= n_first)
    def _():
        o_rest_ref[...] = out


def moe_combine(yb, pos, h, gate, g_final, *, tm, t_first):
    t, d = h.shape
    n_first = t_first // tm
    grid_spec = pltpu.PrefetchScalarGridSpec(
        num_scalar_prefetch=1,
        grid=(t // tm,),
        in_specs=[pl.BlockSpec(memory_space=pl.ANY),
                  pl.BlockSpec((tm, d), lambda i, pos: (i, 0)),
                  pl.BlockSpec((tm, LANES), lambda i, pos: (i, 0)),
                  pl.BlockSpec((1, d), lambda i, pos: (0, 0))],
        out_specs=(pl.BlockSpec((tm, d), lambda i, pos: (jnp.minimum(i, n_first - 1), 0)),
                   pl.BlockSpec((tm, d), lambda i, pos: (jnp.maximum(i - n_first, 0), 0))),
        scratch_shapes=[pltpu.VMEM((2, TOP_K, tm, d), F32), pltpu.SemaphoreType.DMA((2,))],
    )
    assert (tm * TOP_K) % DMA_UNROLL == 0 and DMA_UNROLL % TOP_K == 0
    return pl.pallas_call(
        functools.partial(_combine_kernel, tm=tm, n_first=n_first, n_blocks=t // tm),
        out_shape=(jax.ShapeDtypeStruct((t_first, d), F32), jax.ShapeDtypeStruct((t - t_first, d), F32)),
        grid_spec=grid_spec,
        compiler_params=_params("arbitrary"),
        name="moe_combine",
    )(pos, yb, h, gate, g_final.reshape(1, d))


def _ret_qk_kernel(x_ref, w_ref, cos_ref, sin_ref, o_ref, *, heads_per_step, dk):
    r = jnp.dot(x_ref[...], w_ref[...], preferred_element_type=F32)
    cos, sin = cos_ref[...], sin_ref[...]
    half = dk // 2
    for h in range(heads_per_step):
        e = r[:, h * dk:h * dk + half]
        o = r[:, h * dk + half:(h + 1) * dk]
        o_ref[:, h * dk:h * dk + half] = (e * cos - o * sin).astype(o_ref.dtype)
        o_ref[:, h * dk + half:(h + 1) * dk] = (o * cos + e * sin).astype(o_ref.dtype)


def ret_qk(xn, w_qk, cos_t, sin_t, *, dk):
    t, d = xn.shape
    n = w_qk.shape[1]
    tm = _tile(t, (640, 512, 256, 128, 64, 32, 16, 8))
    tn = _tile(n, (1024, 512, 256))
    return pl.pallas_call(
        functools.partial(_ret_qk_kernel, heads_per_step=tn // dk, dk=dk),
        out_shape=jax.ShapeDtypeStruct((t, n), BF16),
        grid=(n // tn, t // tm),
        in_specs=[pl.BlockSpec((tm, d), lambda j, i: (i, 0)), pl.BlockSpec((d, tn), lambda j, i: (0, j)),
                  pl.BlockSpec((tm, dk // 2), lambda j, i: (i, 0)),
                  pl.BlockSpec((tm, dk // 2), lambda j, i: (i, 0))],
        out_specs=pl.BlockSpec((tm, tn), lambda j, i: (i, j)),
        compiler_params=_params("parallel", "parallel"),
        name="ret_qk",
    )(xn, w_qk, cos_t, sin_t)


def _ret_kernel(q_ref, k_ref, v_ref, g_ref, s0_ref, dmat_ref, qdec_ref, kdec_ref, o_ref, s_out_ref, s_scr,
                *, nc, hps, dk, dv):
    c = pl.program_id(2)

    @pl.when(c == 0)
    def _():
        s_scr[...] = s0_ref[...]

    for h in range(hps):
        ks, vs = slice(h * dk, (h + 1) * dk), slice(h * dv, (h + 1) * dv)
        q, k, v = q_ref[:, ks], k_ref[:, ks], v_ref[:, vs]
        state = s_scr[h]
        qdec = qdec_ref[h][:, 0:1]
        kdec = kdec_ref[h][:, 0:1]
        scores = lax.dot_general(q, k, (((1,), (1,)), ((), ())), preferred_element_type=F32) * dmat_ref[h]
        intra = jnp.dot(scores.astype(BF16), v, preferred_element_type=F32)
        inter = jnp.dot(q, state.astype(BF16), preferred_element_type=F32) * qdec
        o = intra + inter
        kd = (k.astype(F32) * kdec).astype(BF16)
        chunk_decay = qdec[q.shape[0] - 1:, :]
        new_state = chunk_decay * state + lax.dot_general(kd, v, (((0,), (0,)), ((), ())),
                                                          preferred_element_type=F32)
        s_scr[h] = new_state
        g = g_ref[:, vs].astype(F32)
        o_ref[:, vs] = (g * jax.nn.sigmoid(g) * _rms(o)).astype(o_ref.dtype)

    @pl.when(c == nc - 1)
    def _():
        s_out_ref[...] = s_scr[...]


def retention(qk, v, g, s0, log_g, *, batch, heads, length, chunk, row0):
    dk, dv = s0.shape[2], s0.shape[3]
    nc = length // chunk
    blk0 = row0 // chunk
    idx = jnp.arange(chunk, dtype=F32)
    diff = idx[:, None] - idx[None, :]
    dmat = jnp.exp(jnp.where(diff[None] >= 0, diff[None] * log_g[:, None, None], -jnp.inf))
    qdec = jnp.broadcast_to(jnp.exp((idx[None, :] + 1.0) * log_g[:, None])[:, :, None], (heads, chunk, LANES))
    kdec = jnp.broadcast_to(jnp.exp((chunk - 1.0 - idx)[None, :] * log_g[:, None])[:, :, None], (heads, chunk, LANES))
    row = lambda b, h, c: blk0 + b * nc + c
    hps = _tile(heads, (4, 2, 1))
    hg = heads // hps
    return pl.pallas_call(
        functools.partial(_ret_kernel, nc=nc, hps=hps, dk=dk, dv=dv),
        out_shape=(jax.ShapeDtypeStruct((batch * length, heads * dv), BF16),
                   jax.ShapeDtypeStruct((batch, heads, dk, dv), F32)),
        grid=(batch, hg, nc),
        in_specs=[pl.BlockSpec((chunk, hps * dk), lambda b, h, c: (row(b, h, c), h)),
                  pl.BlockSpec((chunk, hps * dk), lambda b, h, c: (row(b, h, c), hg + h)),
                  pl.BlockSpec((chunk, hps * dv), lambda b, h, c: (row(b, h, c), h)),
                  pl.BlockSpec((chunk, hps * dv), lambda b, h, c: (row(b, h, c), h)),
                  pl.BlockSpec((None, hps, dk, dv), lambda b, h, c: (b, h, 0, 0)),
                  pl.BlockSpec((hps, chunk, chunk), lambda b, h, c: (h, 0, 0)),
                  pl.BlockSpec((hps, chunk, LANES), lambda b, h, c: (h, 0, 0)),
                  pl.BlockSpec((hps, chunk, LANES), lambda b, h, c: (h, 0, 0))],
        out_specs=(pl.BlockSpec((chunk, hps * dv), lambda b, h, c: (b * nc + c, h)),
                   pl.BlockSpec((None, hps, dk, dv), lambda b, h, c: (b, h, 0, 0))),
        scratch_shapes=[pltpu.VMEM((hps, dk, dv), F32)],
        compiler_params=_params("parallel", "parallel", "arbitrary"),
        name="retention",
    )(qk, qk, v, g, s0, dmat, qdec, kdec)


def _rope_half_tables(pos):
    half = MLA_ROPE // 2
    inv = 1.0 / (MLA_ROPE_BASE ** (jnp.arange(half, dtype=F32) / half))
    ang = pos.astype(F32)[:, None] * inv[None, :]
    c, s = jnp.cos(ang), jnp.sin(ang)
    z = jnp.zeros((pos.shape[0], LANES - MLA_ROPE), F32)
    return jnp.concatenate([c, c, z], axis=-1), jnp.concatenate([s, s, z], axis=-1)


def _rope_pair_tables(pos, half):
    inv = 1.0 / (RET_ROPE_BASE ** jnp.linspace(0.0, 1.0, half, dtype=F32))
    ang = pos.astype(F32)[:, None] * inv[None, :]
    return jnp.cos(ang), jnp.sin(ang)


def _rot_half_cols(w):
    half = w.shape[-1] // 2
    return jnp.concatenate([-w[..., half:], w[..., :half]], axis=-1)


def _pad_lanes(w):
    return jnp.pad(w, [(0, 0)] * (w.ndim - 1) + [(0, LANES - w.shape[-1])])


def _even_odd_cols(w, heads, dk):
    d = w.shape[0]
    return w.reshape(d, heads, dk // 2, 2).transpose(0, 1, 3, 2).reshape(d, heads * dk)


def _even_odd_rows(s, inverse=False):
    b, h, dk, dv = s.shape
    if inverse:
        return s.reshape(b, h, 2, dk // 2, dv).transpose(0, 1, 3, 2, 4).reshape(b, h, dk, dv)
    return s.reshape(b, h, dk // 2, 2, dv).transpose(0, 1, 3, 2, 4).reshape(b, h, dk, dv)


def _last_active(act):
    return lax.cummax(jnp.where(act > 0, jnp.arange(act.shape[0], dtype=jnp.int32), 0), axis=0)


def _route(idx, n_tok):
    n_assign = n_tok * TOP_K
    flat_e = idx.reshape(-1)
    onehot = (flat_e[:, None] == jnp.arange(N_EXPERTS, dtype=jnp.int32)[None, :]).astype(jnp.int32)
    csum = jnp.cumsum(onehot, axis=0)
    rank = jnp.sum((csum - onehot) * onehot, axis=1)
    counts = csum[-1]
    padded = (counts + MOE_SUPER - 1) // MOE_SUPER * MOE_SUPER
    pad_end = jnp.cumsum(padded)
    pad_start = pad_end - padded
    dest = (jnp.sum(pad_start[None, :] * onehot, axis=1) + rank).astype(jnp.int32)
    n_super = (n_assign + N_EXPERTS * (MOE_SUPER - 1)) // MOE_SUPER
    per_super = MOE_SUPER // MOE_BLOCK
    flat_tok = jnp.arange(n_assign, dtype=jnp.int32) // TOP_K
    slot_tok = jnp.zeros((n_super * MOE_SUPER,), jnp.int32).at[dest].set(flat_tok)
    sup_start = jnp.arange(n_super, dtype=jnp.int32) * MOE_SUPER
    sup_e = jnp.minimum(jnp.sum(sup_start[:, None] >= pad_end[None, :], axis=1), N_EXPERTS - 1).astype(jnp.int32)
    used = jnp.clip(counts[sup_e] - (sup_start - pad_start[sup_e]), 0, MOE_SUPER)
    sup_nact = ((used + MOE_BLOCK - 1) // MOE_BLOCK).astype(jnp.int32)
    blk_e = jnp.repeat(sup_e, per_super)
    blk_act = (jnp.arange(per_super, dtype=jnp.int32)[None, :] < sup_nact[:, None]).astype(jnp.int32).reshape(-1)
    return slot_tok, dest, (sup_e, sup_nact, _last_active(sup_nact)), (blk_e, blk_act, _last_active(blk_act))


def kernel(x_prompt, x_sample, mem_prompt, cache_mla_ckv, cache_mla_krope, state_ret, cache_mem_k, cache_mem_v, norm_mix, norm_xattn, norm_ffn, norm_final, mla_w_dq, mla_g_q, mla_w_uq, mla_w_dkv, mla_g_kv, mla_w_uk, mla_w_uv, mla_w_o, ret_w_q, ret_w_k, ret_w_v, ret_w_g, ret_w_o, xa_w_q, xa_w_k, xa_w_v, xa_w_o, ffn_w1, ffn_w3, ffn_w2, moe_w_router, moe_w1, moe_w3, moe_w2):
    bp, lp, d = x_prompt.shape
    bs, ls, _ = x_sample.shape
    past = cache_mla_ckv.shape[2]
    n_mem = mem_prompt.shape[1]
    tp, ts = bp * lp, bs * ls
    t = tp + ts
    heads = mla_w_uk.shape[2]
    q_rank, kv_rank = mla_w_dq.shape[2], mla_w_uk.shape[1]
    dk = d // RET_HEADS
    dv = 2 * dk
    xa_dim = XA_HEADS * XA_HEAD_DIM

    pos = jnp.concatenate([jnp.tile(jnp.arange(lp, dtype=jnp.int32), bp),
                           jnp.tile(past + jnp.arange(ls, dtype=jnp.int32), bs)])
    x_p, x_s = x_prompt.reshape(tp, d), x_sample.reshape(ts, d)
    tm_x = _tile(lp, (128, 64, 32))

    def cross_attention(h_p, h_s, layer, k_p, v_p, xn_dtype):
        wq, wo = xa_w_q[layer].astype(BF16), xa_w_o[layer].astype(BF16)
        k_all = jnp.concatenate([k_p, cache_mem_k[layer].reshape(bs * n_mem, xa_dim)], axis=0).astype(BF16)
        v_all = jnp.concatenate([v_p, cache_mem_v[layer].reshape(bs * n_mem, xa_dim)], axis=0).astype(BF16)
        return xattn_block(h_p, h_s, norm_xattn[layer], wq, k_all, v_all, wo, norm_ffn[layer], n_long=bp,
                           len_long=lp, len_short=ls, n_mem=n_mem, tm=tm_x, xn_dtype=xn_dtype)

    def memory_kv(layer):
        w = jnp.concatenate([xa_w_k[layer], xa_w_v[layer]], axis=1).astype(BF16)
        kv = matmul(mem_prompt.reshape(bp * n_mem, d).astype(BF16), w, out_dtype=F32, name="mem_kv")
        return kv[:, :xa_dim], kv[:, xa_dim:]

    xn = rmsnorm((x_p, x_s), norm_mix[0], BF16)
    cos_m, sin_m = _rope_half_tables(pos)
    w_dkv = mla_w_dkv[0]
    w_kr = w_dkv[:, kv_rank:]
    w_pre = jnp.concatenate([mla_w_dq[0], w_dkv[:, :kv_rank], _pad_lanes(w_kr), _pad_lanes(_rot_half_cols(w_kr))],
                            axis=1).astype(BF16)
    cq, ckv, ckv_b, kr, kr_b = mla_pre(xn, w_pre, mla_g_q[0], mla_g_kv[0], cos_m, sin_m)

    w_uq = mla_w_uq[0].reshape(q_rank, heads, MLA_NOPE + MLA_ROPE)
    wn = w_uq[:, :, :MLA_NOPE].reshape(q_rank, heads * LANES).astype(BF16)
    w_rope = w_uq[:, :, MLA_NOPE:]
    wr = jnp.concatenate([w_rope, _rot_half_cols(w_rope)], axis=-1).reshape(q_rank, heads * LANES).astype(BF16)
    cs_m = jnp.concatenate([cos_m[:, :MLA_ROPE], sin_m[:, :MLA_ROPE]], axis=-1)
    q = mla_q(cq, wn, wr, cs_m, scale=float((MLA_NOPE + MLA_ROPE) ** -0.5 * 1.4426950408889634))

    w_ukv = jnp.concatenate([mla_w_uk[0].reshape(kv_rank, heads * MLA_NOPE),
                             mla_w_uv[0].reshape(kv_rank, heads * MLA_V)], axis=1).astype(BF16)
    kv = matmul(ckv_b, w_ukv, out_dtype=BF16, name="mla_kv")
    t_att = _tile(lp, (1024, 512, 256, 128, 64))
    o_p = flash_attention(q, kv, kr_b, batch=bp, heads=heads, lq=lp, lk=lp, lk_valid=lp, q_off=0,
                          tq=t_att, tk=t_att)

    lk_s = past + ls
    lk_pad = -(-lk_s // LANES) * LANES
    ckv_all = jnp.concatenate([cache_mla_ckv[0].astype(BF16), ckv_b[tp:].reshape(bs, ls, kv_rank),
                               jnp.zeros((bs, lk_pad - lk_s, kv_rank), BF16)], axis=1)
    kr_all = jnp.concatenate([_pad_lanes(cache_mla_krope[0]).astype(BF16), kr_b[tp:].reshape(bs, ls, LANES),
                              jnp.zeros((bs, lk_pad - lk_s, LANES), BF16)], axis=1)
    kv_s = matmul(ckv_all.reshape(bs * lk_pad, kv_rank), w_ukv, out_dtype=BF16, name="mla_kv_sample")
    o_s = flash_attention(q[tp:], kv_s, kr_all.reshape(bs * lk_pad, LANES), batch=bs, heads=heads, lq=ls,
                          lk=lk_pad, lk_valid=lk_s, q_off=past, tq=ls, tk=lk_pad)
    w_o = mla_w_o[0].astype(BF16)
    h1_p = matmul(o_p, w_o, out_dtype=F32, res=x_p, name="mla_out")
    h1_s = matmul(o_s, w_o, out_dtype=F32, res=x_s, name="mla_out_sample")

    mk0, mv0 = memory_kv(0)
    h2, xn_f = cross_attention(h1_p, h1_s, 0, mk0, mv0, BF16)

    d_ff = ffn_w1.shape[2]
    hf = dense_gateup(xn_f, ffn_w1[0].astype(BF16), ffn_w3[0].astype(BF16),
                      tm=_tile(t, (1280, 640, 256, 128, 64, 32)), tf=_tile(d_ff, (512, 256, 128)))
    h3 = matmul(hf, ffn_w2[0].astype(BF16), out_dtype=F32, res=h2, name="ffn_down")

    xn = rmsnorm(h3, norm_mix[1], BF16)
    cos_r, sin_r = _rope_pair_tables(pos, dk // 2)
    w_qk = jnp.concatenate([_even_odd_cols(ret_w_q[0], RET_HEADS, dk),
                            _even_odd_cols(ret_w_k[0] * (dk ** -0.5), RET_HEADS, dk)], axis=1).astype(BF16)
    qk = ret_qk(xn, w_qk, cos_r, sin_r, dk=dk)
    rv = matmul(xn, ret_w_v[0].astype(BF16), out_dtype=BF16, name="ret_v")
    rg = matmul(xn, ret_w_g[0].astype(BF16), out_dtype=BF16, name="ret_g")
    log_g = jnp.log1p(-jnp.exp2(-5.0 - jnp.arange(RET_HEADS, dtype=F32)))
    ch_p = _tile(lp, (256, 128, 64))
    og_p, st_p = retention(qk, rv, rg, jnp.zeros((bp, RET_HEADS, dk, dv), F32), log_g, batch=bp, heads=RET_HEADS,
                           length=lp, chunk=ch_p, row0=0)
    og_s, st_s = retention(qk, rv, rg, _even_odd_rows(state_ret[0]), log_g, batch=bs, heads=RET_HEADS,
                           length=ls, chunk=ls, row0=tp)
    w_o = ret_w_o[0].astype(BF16)
    h4_p = matmul(og_p, w_o, out_dtype=F32, res=h3, name="ret_out")
    h4_s = matmul(og_s, w_o, out_dtype=F32, res=h3, res_row0=tp, name="ret_out_sample")

    mk1, mv1 = memory_kv(1)
    h5, xn_f = cross_attention(h4_p, h4_s, 1, mk1, mv1, F32)

    idx_l, gate_l = moe_router(xn_f, moe_w_router[0])
    slot_tok, dest, sup, blk = _route(idx_l[:, :TOP_K], t)
    xs = moe_gather(xn_f, slot_tok, blk[1], tm=MOE_BLOCK)
    hs = moe_gateup(xs, moe_w1[0], moe_w3[0], *sup, tm=MOE_SUPER, tf=_tile(d_ff, (512, 256, 128)))
    yb = moe_down(hs, moe_w2[0].astype(BF16), *blk, tm=MOE_BLOCK, tn=_tile(d, (1024, 512, 256, 128)))
    y_p, y_s = moe_combine(yb, dest, h5, gate_l, norm_final, tm=_tile(ts, (128, 64, 32, 16, 8)), t_first=tp)

    def split(a, width):
        return a[:tp].reshape(1, bp, lp, width), a[tp:].reshape(1, bs, ls, width)

    ckv_p, ckv_s = split(ckv, kv_rank)
    kr_p, kr_s = split(kr, MLA_ROPE)
    mem_shape = (bp, n_mem, XA_HEADS, XA_HEAD_DIM)
    return (y_p.reshape(bp, lp, d), y_s.reshape(bs, ls, d),
            ckv_p, kr_p, _even_odd_rows(st_p, inverse=True)[None],
            jnp.stack([mk0.reshape(mem_shape), mk1.reshape(mem_shape)]),
            jnp.stack([mv0.reshape(mem_shape), mv1.reshape(mem_shape)]),
            ckv_s, kr_s, _even_odd_rows(st_s, inverse=True)[None])
```
